```python
import math
import jax, jax.numpy as jnp
from jax import lax
import numpy as np

D_MODEL = 1024
BATCH = 2
SEQ = 16384
DEPTH = 2

N_MIXERS = 2
N_RWKV_LAYERS = (DEPTH + N_MIXERS - 1) // N_MIXERS
N_DIFF_LAYERS = DEPTH // N_MIXERS
NORM_EPS = 1e-6

RW_HEAD = 64
RW_HEADS = D_MODEL // RW_HEAD
RW_WIDTH = RW_HEADS * RW_HEAD
RW_N_PROJ = 4
RW_N_MIX = 6
DECAY_LORA = max(32, int(round(1.8 * D_MODEL ** 0.5 / 32)) * 32)
ICLR_LORA = max(32, int(round(1.8 * D_MODEL ** 0.5 / 32)) * 32)
GN_EPS = 64e-5

DA_HEADS = 8
DA_QK_DIM = D_MODEL // DA_HEADS // 2
DA_V_DIM = 2 * DA_QK_DIM
DA_WIDTH = DA_HEADS * DA_V_DIM
DA_SUBLN_EPS = 1e-5
ROPE_THETA = 10000.0
Q_BLOCK = 128

PLE_DIM = 256

kernel_name = 'rwkv7_diffattn_interleaved_trunk'


def rms_norm(x, g, eps=NORM_EPS):
    xf = x.astype(jnp.float32)
    y = xf * lax.rsqrt(jnp.mean(xf * xf, axis=-1, keepdims=True) + eps)
    return (y * g.astype(jnp.float32)).astype(x.dtype)


def token_shift(x):
    return jnp.pad(x, ((0, 0), (1, 0), (0, 0)))[:, :-1]


def rwkv7_mixer(h, mu, w_in, w0, w1, w2, a0, a1, a2, k_k, k_a, r_k, ln_w, ln_b, w_out):
    B, S, _ = h.shape
    H, N = RW_HEADS, RW_HEAD
    f32 = jnp.float32
    dx = token_shift(h) - h
    xm = h[:, :, None, :] + dx[:, :, None, :] * mu[:RW_N_PROJ]
    proj = jnp.einsum('bsgd,dgc->bsgc', xm, w_in.reshape(D_MODEL, RW_N_PROJ, RW_WIDTH))
    r, k, v, gate = proj[:, :, 0], proj[:, :, 1], proj[:, :, 2], proj[:, :, 3]
    xw = h + dx * mu[4]
    xa = h + dx * mu[5]
    w_log = -jax.nn.softplus(-(w0 + jnp.tanh(xw @ w1) @ w2).astype(f32)) - 0.5
    decay = jnp.exp(-jnp.exp(w_log))
    a = jax.nn.sigmoid((a0 + (xa @ a1) @ a2).astype(f32))
    kk = (k * k_k).reshape(B, S, H, N).astype(f32)
    kk = kk / jnp.maximum(jnp.sqrt(jnp.sum(kk * kk, axis=-1, keepdims=True)), 1e-12)
    k = k.astype(f32) * (1.0 + (a - 1.0) * k_a.astype(f32))

    def heads(t):
        return t.reshape(B, S, H, N).astype(f32)

    rh, kh, vh, wh, ah = heads(r), heads(k), heads(v), heads(decay), heads(a)
    bh = kk * ah

    def step(state, inp):
        r_t, w_t, k_t, v_t, kk_t, b_t = inp
        sa = jnp.einsum('bhij,bhj->bhi', state, kk_t)
        state = (state * w_t[:, :, None, :] - sa[..., None] * b_t[:, :, None, :]
                 + v_t[..., None] * k_t[:, :, None, :])
        y_t = jnp.einsum('bhij,bhj->bhi', state, r_t)
        return state, y_t

    seq_first = lambda t: jnp.swapaxes(t, 0, 1)
    s0 = jnp.zeros((B, H, N, N), f32)
    _, y = lax.scan(step, s0, (seq_first(rh), seq_first(wh), seq_first(kh),
                               seq_first(vh), seq_first(kk), seq_first(bh)))
    y = jnp.swapaxes(y, 0, 1)
    mean = jnp.mean(y, axis=-1, keepdims=True)
    var = jnp.mean(jnp.square(y - mean), axis=-1, keepdims=True)
    y = ((y - mean) * lax.rsqrt(var + GN_EPS) * ln_w.reshape(H, N).astype(f32)
         + ln_b.reshape(H, N).astype(f32))
    bonus = jnp.sum(rh * kh * r_k.astype(f32), axis=-1, keepdims=True) * vh
    y = (y + bonus).reshape(B, S, RW_WIDTH).astype(h.dtype)
    return (y * jax.nn.silu(gate)) @ w_out


def rope(x, cos, sin):
    half = x.shape[-1] // 2
    x1, x2 = x[..., :half], x[..., half:]
    rot = jnp.concatenate([-x2, x1], axis=-1)
    return (x * cos + rot * sin).astype(x.dtype)


def diff_attn_mixer(h, w_in, lq1, lk1, lq2, lk2, subln, w_out, lambda_init):
    B, S, _ = h.shape
    H, dk, dv = DA_HEADS, DA_QK_DIM, DA_V_DIM
    f32 = jnp.float32
    nb = S // Q_BLOCK
    proj = h @ w_in
    q, k, v, gate = jnp.split(proj, [2 * H * dk, 4 * H * dk, 4 * H * dk + H * dv], axis=-1)
    q = q.reshape(B, S, H, 2, dk)
    k = k.reshape(B, S, H, 2, dk)
    v = v.reshape(B, S, H, dv)
    pos = jnp.arange(S, dtype=f32)
    inv_freq = 1.0 / (ROPE_THETA ** (jnp.arange(0, dk, 2, dtype=f32) / dk))
    ang = pos[:, None] * inv_freq[None, :]
    ang = jnp.concatenate([ang, ang], axis=-1)
    cos = jnp.cos(ang)[:, None, None, :]
    sin = jnp.sin(ang)[:, None, None, :]
    q = rope(q, cos, sin)
    k = rope(k, cos, sin)
    lam = (jnp.exp(jnp.sum(lq1.astype(f32) * lk1.astype(f32)))
           - jnp.exp(jnp.sum(lq2.astype(f32) * lk2.astype(f32))) + lambda_init)
    scale = dk ** -0.5
    qb = q.reshape(B, nb, Q_BLOCK, H, 2, dk).transpose(1, 0, 3, 4, 2, 5)
    kt = k.transpose(0, 2, 3, 1, 4)
    vt = v.transpose(0, 2, 1, 3)
    key_pos = jnp.arange(S)

    def block(args):
        i, qi = args
        s = jnp.einsum('bhcqd,bhckd->bhcqk', qi, kt).astype(f32) * scale
        q_pos = i * Q_BLOCK + jnp.arange(Q_BLOCK)
        s = jnp.where(key_pos[None, :] <= q_pos[:, None], s, -jnp.inf)
        pr = jax.nn.softmax(s, axis=-1)
        att = pr[:, :, 0] - lam * pr[:, :, 1]
        return jnp.einsum('bhqk,bhkd->bhqd', att.astype(vt.dtype), vt)

    o = lax.map(block, (jnp.arange(nb), qb))
    o = o.transpose(1, 0, 3, 2, 4).reshape(B, S, H, dv)
    o = rms_norm(o, subln, DA_SUBLN_EPS) * (1.0 - lambda_init)
    o = o.reshape(B, S, DA_WIDTH)
    return (o * jax.nn.silu(gate)) @ w_out


def setup_inputs(seed: int = 0) -> dict:
    key = jax.random.key(seed)
    ks = iter(jax.random.split(key, 40))
    nrm = lambda shape, s: jax.random.normal(next(ks), shape, jnp.float32) * s
    NA, NB, D, C = N_RWKV_LAYERS, N_DIFF_LAYERS, D_MODEL, RW_WIDTH
    lin = jnp.linspace(0.0, 1.0, C, dtype=jnp.float32)
    return {
        'x': nrm((BATCH, SEQ, D), 1.0),
        'p': nrm((DEPTH, BATCH, SEQ, PLE_DIM), 1.0),
        'rw_norm': 1.0 + nrm((NA, D), 0.05),
        'rw_mu': jax.random.uniform(next(ks), (NA, RW_N_MIX, D), jnp.float32),
        'rw_w_in': nrm((NA, D, RW_N_PROJ * C), D ** -0.5),
        'rw_w0': (-7.0 + 5.0 * lin ** 0.85 + 0.5)[None, :] + nrm((NA, C), 0.1),
        'rw_w1': nrm((NA, D, DECAY_LORA), D ** -0.5),
        'rw_w2': nrm((NA, DECAY_LORA, C), 0.1 * DECAY_LORA ** -0.5),
        'rw_a0': nrm((NA, C), 0.1),
        'rw_a1': nrm((NA, D, ICLR_LORA), D ** -0.5),
        'rw_a2': nrm((NA, ICLR_LORA, C), 0.3 * ICLR_LORA ** -0.5),
        'rw_k_k': 0.85 + nrm((NA, C), 0.05),
        'rw_k_a': 1.0 + nrm((NA, C), 0.05),
        'rw_r_k': nrm((NA, RW_HEADS, RW_HEAD), 0.1),
        'rw_ln_w': 1.0 + nrm((NA, C), 0.05),
        'rw_ln_b': nrm((NA, C), 0.02),
        'rw_w_out': nrm((NA, C, D), C ** -0.5),
        'da_norm': 1.0 + nrm((NB, D), 0.05),
        'da_w_in': nrm((NB, D, 4 * DA_HEADS * DA_QK_DIM + 2 * DA_WIDTH), D ** -0.5),
        'da_lq1': nrm((NB, DA_QK_DIM), 0.1),
        'da_lk1': nrm((NB, DA_QK_DIM), 0.1),
        'da_lq2': nrm((NB, DA_QK_DIM), 0.1),
        'da_lk2': nrm((NB, DA_QK_DIM), 0.1),
        'da_subln': 1.0 + nrm((NB, DA_V_DIM), 0.05),
        'da_w_out': nrm((NB, DA_WIDTH, D), DA_WIDTH ** -0.5),
        'pe_norm': 1.0 + nrm((DEPTH, D), 0.05),
        'pe_w_gate': nrm((DEPTH, D, D), D ** -0.5),
        'pe_w_proj': nrm((DEPTH, PLE_DIM, D), PLE_DIM ** -0.5),
        'final_norm': 1.0 + nrm((D,), 0.05),
    }


def reference(x, p, rw_norm, rw_mu, rw_w_in, rw_w0, rw_w1, rw_w2, rw_a0, rw_a1, rw_a2,
              rw_k_k, rw_k_a, rw_r_k, rw_ln_w, rw_ln_b, rw_w_out,
              da_norm, da_w_in, da_lq1, da_lk1, da_lq2, da_lk2, da_subln, da_w_out,
              pe_norm, pe_w_gate, pe_w_proj, final_norm):
    h = x
    for i in range(DEPTH):
        j = i // N_MIXERS
        if i % N_MIXERS == 0:
            h = h + rwkv7_mixer(rms_norm(h, rw_norm[j]), rw_mu[j], rw_w_in[j], rw_w0[j], rw_w1[j],
                                rw_w2[j], rw_a0[j], rw_a1[j], rw_a2[j], rw_k_k[j], rw_k_a[j],
                                rw_r_k[j], rw_ln_w[j], rw_ln_b[j], rw_w_out[j])
        else:
            lambda_init = 0.8 - 0.6 * math.exp(-0.3 * i)
            h = h + diff_attn_mixer(rms_norm(h, da_norm[j]), da_w_in[j], da_lq1[j], da_lk1[j],
                                    da_lq2[j], da_lk2[j], da_subln[j], da_w_out[j], lambda_init)
        g = jax.nn.sigmoid(rms_norm(h, pe_norm[i]) @ pe_w_gate[i])
        h = h + g * (p[i] @ pe_w_proj[i])
    return rms_norm(h, final_norm)
```

```python
import functools
import math

import jax
import jax.numpy as jnp
import numpy as np
from jax import lax
from jax.experimental import pallas as pl
from jax.experimental.pallas import tpu as pltpu

F32 = jnp.float32
BF16 = jnp.bfloat16

LANES = 128
NORM_EPS = 1e-6
GN_EPS = 64e-5
DA_SUBLN_EPS = 1e-5
ROPE_THETA = 10000.0
RW_HEAD = 64
DA_QK_DIM = 64
DA_HEADS = 8
LORA = 64
SCAN_CHUNK = 64
VMEM_LIMIT = 56 * 1024 * 1024
EXP_NEG_HALF = math.exp(-0.5)


def _cparams(sem):
    return pltpu.CompilerParams(dimension_semantics=sem, vmem_limit_bytes=VMEM_LIMIT)


def _rms(x, g, eps):
    return x * lax.rsqrt(jnp.mean(x * x, axis=-1, keepdims=True) + eps) * g


def _sigmoid(x):
    return 1.0 / (1.0 + jnp.exp(-x))


def _bdot(a, b):
    return jnp.dot(a.astype(BF16), b.astype(BF16), preferred_element_type=F32)


def _rwkv_in_kernel(x_ref, xp_ref, nrm_ref, mu_ref, win_ref, w0_ref, w1_ref, w2_ref,
                    a0_ref, a1_ref, a2_ref,
                    r_ref, k_ref, v_ref, g_ref, lw_ref, a_ref):
    i = pl.program_id(1)
    tm = x_ref.shape[1]
    c = r_ref.shape[2]
    nrm = nrm_ref[...]
    hn = _rms(x_ref[0], nrm, NORM_EPS)
    prev_row = _rms(xp_ref[0], nrm, NORM_EPS)[7:8, :]
    prev_row = jnp.where(i == 0, 0.0, prev_row)
    rows = lax.broadcasted_iota(jnp.int32, hn.shape, 0)
    shifted = jnp.where(rows == 0, prev_row, pltpu.roll(hn, 1, 0))
    dx = shifted - hn
    mu = mu_ref[...]

    def lerp(j):
        return (hn + dx * mu[j:j + 1, :]).astype(BF16)

    r_ref[0] = jnp.dot(lerp(0), win_ref[:, 0 * c:1 * c], preferred_element_type=F32)
    k_ref[0] = jnp.dot(lerp(1), win_ref[:, 1 * c:2 * c], preferred_element_type=F32)
    v_ref[0] = jnp.dot(lerp(2), win_ref[:, 2 * c:3 * c], preferred_element_type=F32)
    g_ref[0] = jnp.dot(lerp(3), win_ref[:, 3 * c:4 * c], preferred_element_type=F32)
    tw = jnp.tanh(jnp.dot(lerp(4), w1_ref[...], preferred_element_type=F32))
    zw = w0_ref[...] + _bdot(tw, w2_ref[...])
    lw_ref[0] = -EXP_NEG_HALF * _sigmoid(zw)
    ta = jnp.dot(lerp(5), a1_ref[...], preferred_element_type=F32)
    a_ref[0] = _sigmoid(a0_ref[...] + _bdot(ta, a2_ref[...]))


def _rwkv_in(x, nrm, mu, win, w0, w1, w2, a0, a1, a2, tm):
    b, s, d = x.shape
    c = win.shape[1] // 4
    full = lambda arr: pl.BlockSpec(arr.shape, lambda bi, i: (0,) * arr.ndim)
    out_sds = jax.ShapeDtypeStruct((b, s, c), F32)
    out_spec = pl.BlockSpec((1, tm, c), lambda bi, i: (bi, i, 0))
    return pl.pallas_call(
        _rwkv_in_kernel,
        grid=(b, s // tm),
        in_specs=[
            pl.BlockSpec((1, tm, d), lambda bi, i: (bi, i, 0)),
            pl.BlockSpec((1, 8, d), lambda bi, i: (bi, jnp.maximum(i * (tm // 8) - 1, 0), 0)),
            full(nrm), full(mu), full(win), full(w0), full(w1), full(w2), full(a0), full(a1), full(a2),
        ],
        out_specs=[out_spec] * 6,
        out_shape=[out_sds] * 6,
        compiler_params=_cparams(("parallel", "parallel")),
        name="rwkv_in",
    )(x, x, nrm, mu, win, w0, w1, w2, a0, a1, a2)


def _fdot(a, b):
    return jnp.dot(a, b, precision=lax.Precision.HIGHEST, preferred_element_type=F32)


def _fdot_nt(a, b):
    return lax.dot_general(a, b, (((1,), (1,)), ((), ())), precision=lax.Precision.HIGHEST,
                           preferred_element_type=F32)


def _fdot_tn(a, b):
    return lax.dot_general(a, b, (((0,), (0,)), ((), ())), precision=lax.Precision.HIGHEST,
                           preferred_element_type=F32)


def _scan_kernel(r_ref, k_ref, v_ref, lw_ref, a_ref, kk_ref, ka_ref, rk_ref, lnw_ref, lnb_ref,
                 y_ref, st_ref):
    t = r_ref.shape[1]
    c = r_ref.shape[2]
    npairs = c // LANES
    t2 = 2 * t

    @pl.when(pl.program_id(1) == 0)
    def _():
        st_ref[...] = jnp.zeros_like(st_ref)

    r = r_ref[0]
    kr = k_ref[0]
    v = v_ref[0]
    lw = lw_ref[0]
    a = a_ref[0]

    row_t = lax.broadcasted_iota(jnp.int32, (t, t), 0)
    col_t = lax.broadcasted_iota(jnp.int32, (t, t), 1)
    tri = (row_t >= col_t).astype(F32)
    cum = _fdot(tri, lw)
    cum_last = cum[t - 1:t, :]
    e_cum = jnp.exp(cum)
    e_prev = jnp.exp(cum - lw)
    e_inv = jnp.exp(-cum)
    e_tail = jnp.exp(cum_last - cum)
    e_last = jnp.exp(cum_last)
    kku = kr * kk_ref[...]
    km = kr * (1.0 + (a - 1.0) * ka_ref[...])
    rkr = r * km * rk_ref[...]
    r_dec = r * e_cum

    lane = lax.broadcasted_iota(jnp.int32, (1, LANES), 1)
    first = lane < RW_HEAD
    rl = lax.broadcasted_iota(jnp.int32, (LANES, LANES), 0)
    cl = lax.broadcasted_iota(jnp.int32, (LANES, LANES), 1)
    same_head = (rl >> 6) == (cl >> 6)
    head_ones = same_head.astype(F32)

    r2 = lax.broadcasted_iota(jnp.int32, (t2, t2), 0)
    c2 = lax.broadcasted_iota(jnp.int32, (t2, t2), 1)
    strict = (r2 & (t - 1)) > (c2 & (t - 1))
    blk16 = (r2 >> 4) == (c2 >> 4)
    blk32 = (r2 >> 5) == (c2 >> 5)
    eye = (r2 == c2).astype(F32)
    r1 = lax.broadcasted_iota(jnp.int32, (t, t2), 0)
    c1 = lax.broadcasted_iota(jnp.int32, (t, t2), 1)
    incl = r1 >= (c1 & (t - 1))

    def stack(x):
        return jnp.concatenate([jnp.where(first, x, 0.0), jnp.where(first, 0.0, x)], axis=0)

    for p in range(npairs):
        sl = slice(p * LANES, (p + 1) * LANES)
        ht = st_ref[p]
        kku_p = kku[:, sl]
        sums = _fdot(jnp.concatenate([kku_p * kku_p, rkr[:, sl]], axis=0), head_ones)
        kk = kku_p / jnp.maximum(jnp.sqrt(sums[:t]), 1e-12)
        bonus = sums[t:] * v[:, sl]
        bb = kk * a[:, sl]
        kk_dec = kk * e_prev[:, sl]
        k_hat = km[:, sl] * e_inv[:, sl]
        b_hat = bb * e_inv[:, sl]
        k_bar = km[:, sl] * e_tail[:, sl]
        b_bar = bb * e_tail[:, sl]
        v_p = v[:, sl]
        r_p = r_dec[:, sl]

        kk_st = stack(kk_dec)
        v_st = stack(v_p)
        g = _fdot_nt(jnp.concatenate([kk_st, r_p], axis=0),
                     jnp.concatenate([stack(k_hat), stack(b_hat)], axis=0))
        a_kk = jnp.where(strict, g[:t2, :t2], 0.0)
        a_kb = jnp.where(strict, g[:t2, t2:], 0.0)
        a_rk = jnp.where(incl, g[t2:, :t2], 0.0)
        a_rb = jnp.where(incl, g[t2:, t2:], 0.0)

        n1 = jnp.where(blk16, a_kb, 0.0)
        n2 = _fdot(n1, n1)
        n4 = _fdot(n2, n2)
        n8 = _fdot(n4, n4)
        m = eye - n1
        m = m + _fdot(m, n2)
        m = m + _fdot(m, n4)
        m = m + _fdot(m, n8)
        e1 = jnp.where(blk32 & jnp.logical_not(blk16), a_kb, 0.0)
        m = m - _fdot(_fdot(m, e1), m)
        e2 = jnp.where(blk32, 0.0, a_kb)
        m = m - _fdot(_fdot(m, e2), m)

        av = _fdot(a_kk, v_st)
        w = _fdot(m, jnp.concatenate([av, kk_st], axis=1))
        u_st = w[:, :LANES] + _fdot_nt(w[:, LANES:], ht)
        u = u_st[:t] + u_st[t:]
        y = _fdot_nt(r_p, ht) + _fdot(jnp.concatenate([a_rk, -a_rb], axis=1),
                                      jnp.concatenate([v_st, u_st], axis=0))
        upd = _fdot_tn(jnp.concatenate([v_p, u], axis=0), jnp.concatenate([k_bar, -b_bar], axis=0))
        st_ref[p] = ht * e_last[:, sl] + jnp.where(same_head, upd, 0.0)

        mean = _fdot(y, head_ones) * (1.0 / RW_HEAD)
        d = y - mean
        var = _fdot(d * d, head_ones) * (1.0 / RW_HEAD)
        yn = d * lax.rsqrt(var + GN_EPS) * lnw_ref[:, sl] + lnb_ref[:, sl]
        y_ref[0, :, sl] = yn + bonus


def _rwkv_scan(r, k, v, lw, a, k_k, k_a, r_k, ln_w, ln_b):
    b, s, c = r.shape
    t = SCAN_CHUNK
    seq = pl.BlockSpec((1, t, c), lambda bi, i: (bi, i, 0))
    par = pl.BlockSpec((1, c), lambda bi, i: (0, 0))
    return pl.pallas_call(
        _scan_kernel,
        grid=(b, s // t),
        in_specs=[seq] * 5 + [par] * 5,
        out_specs=seq,
        out_shape=jax.ShapeDtypeStruct((b, s, c), F32),
        scratch_shapes=[pltpu.VMEM((c // LANES, LANES, LANES), F32)],
        compiler_params=_cparams(("parallel", "arbitrary")),
        name="rwkv_scan",
    )(r, k, v, lw, a, k_k, k_a, r_k, ln_w, ln_b)


def _ple(h, p, nrm, wg, wp):
    g = _sigmoid(_bdot(_rms(h, nrm, NORM_EPS), wg))
    return h + g * _bdot(p, wp)


def _mid_kernel(x_ref, y_ref, g_ref, p_ref, wo_ref, pn_ref, pg_ref, pp_ref, dn_ref, din_ref,
                cos_ref, sin_ref,
                h_ref, qa_ref, qb_ref, k_ref, v_ref, go_ref):
    nh = qa_ref.shape[1]
    gate = g_ref[0]
    z = y_ref[0] * (gate * _sigmoid(gate))
    h = x_ref[0] + _bdot(z, wo_ref[...])
    h = _ple(h, p_ref[0], pn_ref[...], pg_ref[...], pp_ref[...])
    h_ref[0] = h
    proj = _bdot(_rms(h, dn_ref[...], NORM_EPS), din_ref[...])
    cw = nh * LANES
    cos = cos_ref[...]
    sin = sin_ref[...]
    lane = lax.broadcasted_iota(jnp.int32, (1, LANES), 1)
    low = (lane & (DA_QK_DIM - 1)) < (DA_QK_DIM // 2)
    first = lane < DA_QK_DIM
    scale = DA_QK_DIM ** -0.5

    def rope(xh):
        rot = jnp.where(low, pltpu.roll(xh, LANES - DA_QK_DIM // 2, 1), pltpu.roll(xh, DA_QK_DIM // 2, 1))
        return xh * cos + rot * sin

    for hd in range(nh):
        sl = slice(hd * LANES, (hd + 1) * LANES)
        q = rope(proj[:, sl]) * scale
        qa_ref[0, hd] = jnp.where(first, q, 0.0).astype(BF16)
        qb_ref[0, hd] = jnp.where(first, 0.0, q).astype(BF16)
        k_ref[0, hd] = rope(proj[:, cw + hd * LANES:cw + (hd + 1) * LANES]).astype(BF16)
        v_ref[0, hd] = proj[:, 2 * cw + hd * LANES:2 * cw + (hd + 1) * LANES].astype(BF16)
    go_ref[0] = proj[:, 3 * cw:]


def _mid(x, y, g, p, wo, pn, pg, pp, dn, din, cos, sin, tm):
    b, s, d = x.shape
    nh = DA_HEADS
    pd = p.shape[2]
    full = lambda arr: pl.BlockSpec(arr.shape, lambda bi, i: (0,) * arr.ndim)
    row = lambda w: pl.BlockSpec((1, tm, w), lambda bi, i: (bi, i, 0))
    hd_spec = pl.BlockSpec((1, nh, tm, LANES), lambda bi, i: (bi, 0, i, 0))
    hd_sds = jax.ShapeDtypeStruct((b, nh, s, LANES), BF16)
    tab = pl.BlockSpec((tm, LANES), lambda bi, i: (i, 0))
    return pl.pallas_call(
        _mid_kernel,
        grid=(b, s // tm),
        in_specs=[row(d), row(d), row(d), row(pd), full(wo), full(pn), full(pg), full(pp), full(dn),
                  full(din), tab, tab],
        out_specs=[row(d), hd_spec, hd_spec, hd_spec, hd_spec, row(d)],
        out_shape=[jax.ShapeDtypeStruct((b, s, d), F32), hd_sds, hd_sds, hd_sds, hd_sds,
                   jax.ShapeDtypeStruct((b, s, d), F32)],
        compiler_params=_cparams(("parallel", "parallel")),
        name="mid",
    )(x, y, g, p, wo, pn, pg, pp, dn, din, cos, sin)


def _attn_kernel(qi_ref, ki_ref, qa_ref, qb_ref, k_ref, v_ref, lq1_ref, lk1_ref, lq2_ref, lk2_ref,
                 sub_ref, o_ref, m1_ref, l1_ref, acc1_ref, m2_ref, l2_ref, acc2_ref, *, lambda_init):
    step = pl.program_id(2)
    qi = qi_ref[step]
    ki = ki_ref[step]
    tq = qa_ref.shape[2]
    tk = k_ref.shape[2]

    @pl.when(ki == 0)
    def _():
        m1_ref[...] = jnp.full_like(m1_ref, -jnp.inf)
        m2_ref[...] = jnp.full_like(m2_ref, -jnp.inf)
        l1_ref[...] = jnp.zeros_like(l1_ref)
        l2_ref[...] = jnp.zeros_like(l2_ref)
        acc1_ref[...] = jnp.zeros_like(acc1_ref)
        acc2_ref[...] = jnp.zeros_like(acc2_ref)

    k = k_ref[0, 0]
    v = v_ref[0, 0]

    def update(q_ref, m_ref, l_ref, acc_ref, masked):
        s = lax.dot_general(q_ref[0, 0], k, (((1,), (1,)), ((), ())), preferred_element_type=F32)
        if masked:
            rq = lax.broadcasted_iota(jnp.int32, (tq, tk), 0)
            ck = lax.broadcasted_iota(jnp.int32, (tq, tk), 1)
            s = jnp.where(ck <= rq, s, -jnp.inf)
        m_old = m_ref[...]
        m_new = jnp.maximum(m_old, jnp.max(s, axis=-1, keepdims=True))
        alpha = jnp.exp(m_old - m_new)
        pr = jnp.exp(s - m_new[:, :1])
        l_ref[...] = alpha * l_ref[...] + jnp.sum(pr, axis=-1, keepdims=True)
        acc_ref[...] = alpha * acc_ref[...] + jnp.dot(pr.astype(BF16), v, preferred_element_type=F32)
        m_ref[...] = m_new

    @pl.when(ki < qi)
    def _():
        update(qa_ref, m1_ref, l1_ref, acc1_ref, False)
        update(qb_ref, m2_ref, l2_ref, acc2_ref, False)

    @pl.when(ki == qi)
    def _():
        update(qa_ref, m1_ref, l1_ref, acc1_ref, True)
        update(qb_ref, m2_ref, l2_ref, acc2_ref, True)
        lam = (jnp.exp(jnp.sum(lq1_ref[...] * lk1_ref[...], axis=-1, keepdims=True))
               - jnp.exp(jnp.sum(lq2_ref[...] * lk2_ref[...], axis=-1, keepdims=True)) + lambda_init)
        o = acc1_ref[...] / l1_ref[...] - lam * (acc2_ref[...] / l2_ref[...])
        o_ref[0] = _rms(o, sub_ref[...], DA_SUBLN_EPS) * (1.0 - lambda_init)


def _attn(qa, qb, k, v, lq1, lk1, lq2, lk2, sub, lambda_init, tile):
    b, nh, s, dv = v.shape
    nq = s // tile
    pairs = [(q, kk) for q in range(nq) for kk in range(q + 1)]
    qi = jnp.asarray(np.array([pq for pq, _ in pairs], np.int32))
    ki = jnp.asarray(np.array([pk for _, pk in pairs], np.int32))
    qspec = pl.BlockSpec((1, 1, tile, LANES), lambda bi, h, t, qi, ki: (bi, h, qi[t], 0))
    kspec = pl.BlockSpec((1, 1, tile, LANES), lambda bi, h, t, qi, ki: (bi, h, ki[t], 0))
    small = lambda arr: pl.BlockSpec(arr.shape, lambda bi, h, t, qi, ki: (0,) * arr.ndim)
    stat = pltpu.VMEM((tile, LANES), F32)
    grid_spec = pltpu.PrefetchScalarGridSpec(
        num_scalar_prefetch=2,
        grid=(b, nh, len(pairs)),
        in_specs=[qspec, qspec, kspec, kspec, small(lq1), small(lk1), small(lq2), small(lk2), small(sub)],
        out_specs=pl.BlockSpec((1, tile, LANES), lambda bi, h, t, qi, ki: (bi, qi[t], h)),
        scratch_shapes=[stat] * 6,
    )
    return pl.pallas_call(
        functools.partial(_attn_kernel, lambda_init=lambda_init),
        grid_spec=grid_spec,
        out_shape=jax.ShapeDtypeStruct((b, s, nh * dv), F32),
        compiler_params=_cparams(("parallel", "parallel", "arbitrary")),
        name="diff_attn",
    )(qi, ki, qa, qb, k, v, lq1, lk1, lq2, lk2, sub)


def _tail_kernel(h_ref, o_ref, g_ref, p_ref, wo_ref, pn_ref, pg_ref, pp_ref, fn_ref, out_ref):
    gate = g_ref[0]
    z = o_ref[0] * (gate * _sigmoid(gate))
    h = h_ref[0] + _bdot(z, wo_ref[...])
    h = _ple(h, p_ref[0], pn_ref[...], pg_ref[...], pp_ref[...])
    out_ref[0] = _rms(h, fn_ref[...], NORM_EPS)


def _tail(h, o, g, p, wo, pn, pg, pp, fn, tm):
    b, s, d = h.shape
    pd = p.shape[2]
    full = lambda arr: pl.BlockSpec(arr.shape, lambda bi, i: (0,) * arr.ndim)
    row = lambda w: pl.BlockSpec((1, tm, w), lambda bi, i: (bi, i, 0))
    return pl.pallas_call(
        _tail_kernel,
        grid=(b, s // tm),
        in_specs=[row(d), row(d), row(d), row(pd), full(wo), full(pn), full(pg), full(pp), full(fn)],
        out_specs=row(d),
        out_shape=jax.ShapeDtypeStruct((b, s, d), F32),
        compiler_params=_cparams(("parallel", "parallel")),
        name="tail",
    )(h, o, g, p, wo, pn, pg, pp, fn)


def _rope_tables(s):
    dk = DA_QK_DIM
    pos = jnp.arange(s, dtype=F32)
    inv_freq = 1.0 / (ROPE_THETA ** (jnp.arange(0, dk, 2, dtype=F32) / dk))
    ang = pos[:, None] * inv_freq[None, :]
    ang = jnp.concatenate([ang, ang], axis=-1)
    cos = jnp.cos(ang)
    sin = jnp.sin(ang)
    sign = jnp.where(jnp.arange(dk) < dk // 2, -1.0, 1.0).astype(F32)
    sin = sin * sign[None, :]
    return jnp.concatenate([cos, cos], axis=-1), jnp.concatenate([sin, sin], axis=-1)


def kernel(x, p, rw_norm, rw_mu, rw_w_in, rw_w0, rw_w1, rw_w2, rw_a0, rw_a1, rw_a2, rw_k_k, rw_k_a, rw_r_k, rw_ln_w, rw_ln_b, rw_w_out, da_norm, da_w_in, da_lq1, da_lk1, da_lq2, da_lk2, da_subln, da_w_out, pe_norm, pe_w_gate, pe_w_proj, final_norm):
    b, s, d = x.shape
    assert p.shape[0] == 2 and rw_norm.shape[0] == 1 and da_norm.shape[0] == 1
    tm = min(256, s)
    tile = min(512, s)
    row = lambda vec: vec.reshape(1, -1)
    bf = lambda w: w.astype(BF16)

    r, k, v, gate, lw, a = _rwkv_in(
        x, row(rw_norm[0]), rw_mu[0], bf(rw_w_in[0]), row(rw_w0[0]), bf(rw_w1[0]), bf(rw_w2[0]),
        row(rw_a0[0]), bf(rw_a1[0]), bf(rw_a2[0]), tm)
    y = _rwkv_scan(r, k, v, lw, a, row(rw_k_k[0]), row(rw_k_a[0]), row(rw_r_k[0]), row(rw_ln_w[0]),
                   row(rw_ln_b[0]))
    cos, sin = _rope_tables(s)
    h1, qa, qb, kq, vq, gate2 = _mid(
        x, y, gate, p[0], bf(rw_w_out[0]), row(pe_norm[0]), bf(pe_w_gate[0]), bf(pe_w_proj[0]),
        row(da_norm[0]), bf(da_w_in[0]), cos, sin, tm)
    lambda_init = 0.8 - 0.6 * math.exp(-0.3 * 1)
    o = _attn(qa, qb, kq, vq, row(da_lq1[0]), row(da_lk1[0]), row(da_lq2[0]), row(da_lk2[0]),
              row(da_subln[0]), lambda_init, tile)
    return _tail(h1, o, gate2, p[1], bf(da_w_out[0]), row(pe_norm[1]), bf(pe_w_gate[1]),
                 bf(pe_w_proj[1]), row(final_norm), tm)
```

```python
import functools
import math

import jax
import jax.numpy as jnp
import numpy as np
from jax import lax
from jax.experimental import pallas as pl
from jax.experimental.pallas import tpu as pltpu

F32 = jnp.float32
BF16 = jnp.bfloat16

LANES = 128
NORM_EPS = 1e-6
GN_EPS = 64e-5
DA_SUBLN_EPS = 1e-5
ROPE_THETA = 10000.0
RW_HEAD = 64
DA_QK_DIM = 64
DA_HEADS = 8
LORA = 64
SCAN_CHUNK = 64
ATTN_TILE = 1024
ATTN_ROW_BLOCK = 128
VMEM_LIMIT = 56 * 1024 * 1024
EXP_NEG_HALF = math.exp(-0.5)


def _cparams(sem):
    return pltpu.CompilerParams(dimension_semantics=sem, vmem_limit_bytes=VMEM_LIMIT)


def _rms(x, g, eps):
    return x * lax.rsqrt(jnp.mean(x * x, axis=-1, keepdims=True) + eps) * g


def _sigmoid(x):
    return 1.0 / (1.0 + jnp.exp(-x))


def _bdot(a, b):
    return jnp.dot(a.astype(BF16), b.astype(BF16), preferred_element_type=F32)


def _rwkv_in_kernel(x_ref, xp_ref, nrm_ref, mu_ref, win_ref, w0_ref, w1_ref, w2_ref,
                    a0_ref, a1_ref, a2_ref,
                    r_ref, k_ref, v_ref, g_ref, lw_ref, a_ref):
    i = pl.program_id(1)
    tm = x_ref.shape[1]
    c = r_ref.shape[2]
    nrm = nrm_ref[...]
    hn = _rms(x_ref[0], nrm, NORM_EPS)
    prev_row = _rms(xp_ref[0], nrm, NORM_EPS)[7:8, :]
    prev_row = jnp.where(i == 0, 0.0, prev_row)
    rows = lax.broadcasted_iota(jnp.int32, hn.shape, 0)
    shifted = jnp.where(rows == 0, prev_row, pltpu.roll(hn, 1, 0))
    dx = shifted - hn
    mu = mu_ref[...]

    def lerp(j):
        return (hn + dx * mu[j:j + 1, :]).astype(BF16)

    r_ref[0] = jnp.dot(lerp(0), win_ref[:, 0 * c:1 * c], preferred_element_type=F32)
    k_ref[0] = jnp.dot(lerp(1), win_ref[:, 1 * c:2 * c], preferred_element_type=F32)
    v_ref[0] = jnp.dot(lerp(2), win_ref[:, 2 * c:3 * c], preferred_element_type=F32)
    g_ref[0] = jnp.dot(lerp(3), win_ref[:, 3 * c:4 * c], preferred_element_type=F32)
    tw = jnp.tanh(jnp.dot(lerp(4), w1_ref[...], preferred_element_type=F32))
    zw = w0_ref[...] + _bdot(tw, w2_ref[...])
    lw_ref[0] = -EXP_NEG_HALF * _sigmoid(zw)
    ta = jnp.dot(lerp(5), a1_ref[...], preferred_element_type=F32)
    a_ref[0] = _sigmoid(a0_ref[...] + _bdot(ta, a2_ref[...]))


def _rwkv_in(x, nrm, mu, win, w0, w1, w2, a0, a1, a2, tm):
    b, s, d = x.shape
    c = win.shape[1] // 4
    full = lambda arr: pl.BlockSpec(arr.shape, lambda bi, i: (0,) * arr.ndim)
    out_sds = jax.ShapeDtypeStruct((b, s, c), F32)
    out_spec = pl.BlockSpec((1, tm, c), lambda bi, i: (bi, i, 0))
    return pl.pallas_call(
        _rwkv_in_kernel,
        grid=(b, s // tm),
        in_specs=[
            pl.BlockSpec((1, tm, d), lambda bi, i: (bi, i, 0)),
            pl.BlockSpec((1, 8, d), lambda bi, i: (bi, jnp.maximum(i * (tm // 8) - 1, 0), 0)),
            full(nrm), full(mu), full(win), full(w0), full(w1), full(w2), full(a0), full(a1), full(a2),
        ],
        out_specs=[out_spec] * 6,
        out_shape=[out_sds] * 6,
        compiler_params=_cparams(("parallel", "parallel")),
        name="rwkv_in",
    )(x, x, nrm, mu, win, w0, w1, w2, a0, a1, a2)


def _hi_lo(x):
    hi = x.astype(BF16)
    lo = (x - hi.astype(F32)).astype(BF16)
    return hi, lo


def _cat2(xs, ys, axis):
    return (jnp.concatenate([xs[0], ys[0]], axis=axis), jnp.concatenate([xs[1], ys[1]], axis=axis))


def _dot3(a, b, dims):
    lhs = jnp.concatenate([a[0], a[1], a[0]], axis=dims[0])
    rhs = jnp.concatenate([b[0], b[0], b[1]], axis=dims[1])
    return lax.dot_general(lhs, rhs, (((dims[0],), (dims[1],)), ((), ())), preferred_element_type=F32)


def _mm(a, b):
    return _dot3(a, b, (1, 0))


def _mm_nt(a, b):
    return _dot3(a, b, (1, 1))


def _mm_tn(a, b):
    return _dot3(a, b, (0, 0))


def _scan_kernel(r_ref, k_ref, v_ref, lw_ref, a_ref, kk_ref, ka_ref, rk_ref, lnw_ref, lnb_ref,
                 y_ref, st_ref):
    t = r_ref.shape[1]
    c = r_ref.shape[2]
    npairs = c // LANES
    t2 = 2 * t

    @pl.when(pl.program_id(1) == 0)
    def _():
        st_ref[...] = jnp.zeros_like(st_ref)

    r = r_ref[0]
    kr = k_ref[0]
    v = v_ref[0]
    lw = lw_ref[0]
    a = a_ref[0]

    row_t = lax.broadcasted_iota(jnp.int32, (t, t), 0)
    col_t = lax.broadcasted_iota(jnp.int32, (t, t), 1)
    tri = (row_t >= col_t).astype(BF16)
    lw_h = lw.astype(BF16)
    lw_r = lw - lw_h.astype(F32)
    lw_m = lw_r.astype(BF16)
    lw_l = (lw_r - lw_m.astype(F32)).astype(BF16)
    cum = (jnp.dot(tri, lw_h, preferred_element_type=F32) + jnp.dot(tri, lw_m, preferred_element_type=F32)
           + jnp.dot(tri, lw_l, preferred_element_type=F32))
    cum_last = cum[t - 1:t, :]
    e_cum = jnp.exp(cum)
    e_prev = jnp.exp(cum - lw)
    e_inv = jnp.exp(-cum)
    e_tail = jnp.exp(cum_last - cum)
    e_last = jnp.exp(cum_last)
    kku = kr * kk_ref[...]
    km = kr * (1.0 + (a - 1.0) * ka_ref[...])
    rkr = r * km * rk_ref[...]
    r_dec = r * e_cum

    lane = lax.broadcasted_iota(jnp.int32, (1, LANES), 1)
    first = lane < RW_HEAD
    rl = lax.broadcasted_iota(jnp.int32, (LANES, LANES), 0)
    cl = lax.broadcasted_iota(jnp.int32, (LANES, LANES), 1)
    same_head = (rl >> 6) == (cl >> 6)
    head_ones = same_head.astype(BF16)
    head_ones2 = jnp.concatenate([head_ones, head_ones], axis=0)

    def head_sum(x):
        xh, xl = _hi_lo(x)
        return jnp.dot(jnp.concatenate([xh, xl], axis=1), head_ones2, preferred_element_type=F32)

    r2 = lax.broadcasted_iota(jnp.int32, (t2, t2), 0)
    c2 = lax.broadcasted_iota(jnp.int32, (t2, t2), 1)
    strict = (r2 & (t - 1)) > (c2 & (t - 1))
    blk16 = (r2 >> 4) == (c2 >> 4)
    blk32 = (r2 >> 5) == (c2 >> 5)
    eye = (r2 == c2).astype(F32)
    r1 = lax.broadcasted_iota(jnp.int32, (t, t2), 0)
    c1 = lax.broadcasted_iota(jnp.int32, (t, t2), 1)
    incl = r1 >= (c1 & (t - 1))

    zero = jnp.zeros((), BF16)

    def stack(x):
        return tuple(jnp.concatenate([jnp.where(first, h, zero), jnp.where(first, zero, h)], axis=0)
                     for h in x)

    prs = range(npairs)
    sls = [slice(p * LANES, (p + 1) * LANES) for p in prs]
    each = lambda f, *xs: [f(*args) for args in zip(*xs)]

    sums = [head_sum(jnp.concatenate([kku[:, sl] * kku[:, sl], rkr[:, sl]], axis=0)) for sl in sls]
    kk = [kku[:, sl] / jnp.maximum(jnp.sqrt(sm[:t]), 1e-12) for sl, sm in zip(sls, sums)]
    bonus = [sm[t:] * v[:, sl] for sl, sm in zip(sls, sums)]
    bb = [kk_p * a[:, sl] for sl, kk_p in zip(sls, kk)]
    v_p = [_hi_lo(v[:, sl]) for sl in sls]
    r_p = [_hi_lo(r_dec[:, sl]) for sl in sls]
    kk_st = [stack(_hi_lo(kk_p * e_prev[:, sl])) for sl, kk_p in zip(sls, kk)]
    v_st = each(stack, v_p)
    kb_hat = [_cat2(stack(_hi_lo(km[:, sl] * e_inv[:, sl])), stack(_hi_lo(bb_p * e_inv[:, sl])), 0)
              for sl, bb_p in zip(sls, bb)]
    kb_bar = [_hi_lo(jnp.concatenate([km[:, sl] * e_tail[:, sl], -(bb_p * e_tail[:, sl])], axis=0))
              for sl, bb_p in zip(sls, bb)]

    g = [_mm_nt(_cat2(kk_st[p], r_p[p], 0), kb_hat[p]) for p in prs]
    a_kk = [_hi_lo(jnp.where(strict, g_p[:t2, :t2], 0.0)) for g_p in g]
    a_kb = [jnp.where(strict, g_p[:t2, t2:], 0.0) for g_p in g]
    a_r = [_hi_lo(jnp.concatenate([jnp.where(incl, g_p[t2:, :t2], 0.0),
                                   -jnp.where(incl, g_p[t2:, t2:], 0.0)], axis=1)) for g_p in g]

    n1 = [jnp.where(blk16, x, 0.0) for x in a_kb]
    n1s = each(_hi_lo, n1)
    n2s = [_hi_lo(_mm(x, x)) for x in n1s]
    n4s = [_hi_lo(_mm(x, x)) for x in n2s]
    n8s = [_hi_lo(_mm(x, x)) for x in n4s]
    m = [eye - x for x in n1]
    for ns in (n2s, n4s, n8s):
        m = [m_p + _mm(_hi_lo(m_p), n_p) for m_p, n_p in zip(m, ns)]
    for sel in (blk32 & jnp.logical_not(blk16), jnp.logical_not(blk32)):
        ms = each(_hi_lo, m)
        es = [_hi_lo(jnp.where(sel, x, 0.0)) for x in a_kb]
        me = [_hi_lo(_mm(ms_p, e_p)) for ms_p, e_p in zip(ms, es)]
        m = [m_p - _mm(me_p, ms_p) for m_p, me_p, ms_p in zip(m, me, ms)]
    ms = each(_hi_lo, m)

    av = [_hi_lo(_mm(a_kk[p], v_st[p])) for p in prs]
    w = [_mm(ms[p], _cat2(av[p], kk_st[p], 1)) for p in prs]
    ht = [_hi_lo(st_ref[p]) for p in prs]
    u_st = [w[p][:, :LANES] + _mm_nt(_hi_lo(w[p][:, LANES:]), ht[p]) for p in prs]
    u = [x[:t] + x[t:] for x in u_st]
    upd = [_mm_tn(_cat2(v_p[p], _hi_lo(u[p]), 0), kb_bar[p]) for p in prs]
    for p in prs:
        st_ref[p] = st_ref[p] * e_last[:, sls[p]] + jnp.where(same_head, upd[p], 0.0)
    y = [_mm_nt(r_p[p], ht[p]) + _mm(a_r[p], _cat2(v_st[p], _hi_lo(u_st[p]), 0)) for p in prs]

    mean = [head_sum(x) * (1.0 / RW_HEAD) for x in y]
    d = [x - mu for x, mu in zip(y, mean)]
    var = [head_sum(x * x) * (1.0 / RW_HEAD) for x in d]
    for p in prs:
        sl = sls[p]
        yn = d[p] * lax.rsqrt(var[p] + GN_EPS) * lnw_ref[:, sl] + lnb_ref[:, sl]
        y_ref[0, :, sl] = yn + bonus[p]


def _rwkv_scan(r, k, v, lw, a, k_k, k_a, r_k, ln_w, ln_b):
    b, s, c = r.shape
    t = SCAN_CHUNK
    seq = pl.BlockSpec((1, t, c), lambda bi, i: (bi, i, 0))
    par = pl.BlockSpec((1, c), lambda bi, i: (0, 0))
    return pl.pallas_call(
        _scan_kernel,
        grid=(b, s // t),
        in_specs=[seq] * 5 + [par] * 5,
        out_specs=seq,
        out_shape=jax.ShapeDtypeStruct((b, s, c), F32),
        scratch_shapes=[pltpu.VMEM((c // LANES, LANES, LANES), F32)],
        compiler_params=_cparams(("parallel", "arbitrary")),
        name="rwkv_scan",
    )(r, k, v, lw, a, k_k, k_a, r_k, ln_w, ln_b)


def _ple(h, p, nrm, wg, wp):
    g = _sigmoid(_bdot(_rms(h, nrm, NORM_EPS), wg))
    return h + g * _bdot(p, wp)


def _mid_kernel(x_ref, y_ref, g_ref, p_ref, wo_ref, pn_ref, pg_ref, pp_ref, dn_ref, din_ref,
                cos_ref, sin_ref,
                h_ref, qa_ref, qb_ref, k_ref, v_ref, go_ref):
    nh = qa_ref.shape[1]
    gate = g_ref[0]
    z = y_ref[0] * (gate * _sigmoid(gate))
    h = x_ref[0] + _bdot(z, wo_ref[...])
    h = _ple(h, p_ref[0], pn_ref[...], pg_ref[...], pp_ref[...])
    h_ref[0] = h
    proj = _bdot(_rms(h, dn_ref[...], NORM_EPS), din_ref[...])
    cw = nh * LANES
    cos = cos_ref[...]
    sin = sin_ref[...]
    lane = lax.broadcasted_iota(jnp.int32, (1, LANES), 1)
    low = (lane & (DA_QK_DIM - 1)) < (DA_QK_DIM // 2)
    first = lane < DA_QK_DIM
    scale = DA_QK_DIM ** -0.5 * math.log2(math.e)
    ones = jnp.ones((proj.shape[0], LANES), BF16)

    def rope(xh):
        rot = jnp.where(low, pltpu.roll(xh, LANES - DA_QK_DIM // 2, 1), pltpu.roll(xh, DA_QK_DIM // 2, 1))
        return xh * cos + rot * sin

    for hd in range(nh):
        sl = slice(hd * LANES, (hd + 1) * LANES)
        q = rope(proj[:, sl]) * scale
        qa_ref[0, hd] = jnp.where(first, q, 0.0).astype(BF16)
        qb_ref[0, hd] = jnp.where(first, 0.0, q).astype(BF16)
        k_ref[0, hd] = rope(proj[:, cw + hd * LANES:cw + (hd + 1) * LANES]).astype(BF16)
        v_ref[0, hd, :, :LANES] = proj[:, 2 * cw + hd * LANES:2 * cw + (hd + 1) * LANES].astype(BF16)
        v_ref[0, hd, :, LANES:] = ones
    go_ref[0] = proj[:, 3 * cw:]


def _mid(x, y, g, p, wo, pn, pg, pp, dn, din, cos, sin, tm):
    b, s, d = x.shape
    nh = DA_HEADS
    pd = p.shape[2]
    full = lambda arr: pl.BlockSpec(arr.shape, lambda bi, i: (0,) * arr.ndim)
    row = lambda w: pl.BlockSpec((1, tm, w), lambda bi, i: (bi, i, 0))
    hd_spec = pl.BlockSpec((1, nh, tm, LANES), lambda bi, i: (bi, 0, i, 0))
    hd_sds = jax.ShapeDtypeStruct((b, nh, s, LANES), BF16)
    va_spec = pl.BlockSpec((1, nh, tm, 2 * LANES), lambda bi, i: (bi, 0, i, 0))
    va_sds = jax.ShapeDtypeStruct((b, nh, s, 2 * LANES), BF16)
    tab = pl.BlockSpec((tm, LANES), lambda bi, i: (i, 0))
    return pl.pallas_call(
        _mid_kernel,
        grid=(b, s // tm),
        in_specs=[row(d), row(d), row(d), row(pd), full(wo), full(pn), full(pg), full(pp), full(dn),
                  full(din), tab, tab],
        out_specs=[row(d), hd_spec, hd_spec, hd_spec, va_spec, row(d)],
        out_shape=[jax.ShapeDtypeStruct((b, s, d), F32), hd_sds, hd_sds, hd_sds, va_sds,
                   jax.ShapeDtypeStruct((b, s, d), F32)],
        compiler_params=_cparams(("parallel", "parallel")),
        name="mid",
    )(x, y, g, p, wo, pn, pg, pp, dn, din, cos, sin)


def _attn_kernel(qi_ref, ki_ref, qa_ref, qb_ref, k_ref, v_ref, lq1_ref, lk1_ref, lq2_ref, lk2_ref,
                 sub_ref, o_ref, m1_ref, acc1_ref, m2_ref, acc2_ref, *, lambda_init, row_block):
    step = pl.program_id(2)
    qi = qi_ref[step]
    ki = ki_ref[step]
    tq = qa_ref.shape[2]
    tk = k_ref.shape[2]

    @pl.when(ki == 0)
    def _():
        m1_ref[...] = jnp.full_like(m1_ref, -jnp.inf)
        m2_ref[...] = jnp.full_like(m2_ref, -jnp.inf)
        acc1_ref[...] = jnp.zeros_like(acc1_ref)
        acc2_ref[...] = jnp.zeros_like(acc2_ref)

    def update(q_ref, m_ref, acc_ref, r0, masked):
        rows = pl.ds(r0, row_block)
        nk = min(r0 + row_block, tk) if masked else tk
        s = lax.dot_general(q_ref[0, 0, rows, :], k_ref[0, 0, :nk, :], (((1,), (1,)), ((), ())),
                            preferred_element_type=F32)
        if masked:
            rq = lax.broadcasted_iota(jnp.int32, (row_block, nk), 0) + r0
            ck = lax.broadcasted_iota(jnp.int32, (row_block, nk), 1)
            s = jnp.where(ck <= rq, s, -jnp.inf)
        chunks = [s[:, c * LANES:(c + 1) * LANES] for c in range(nk // LANES)]
        m_tile = functools.reduce(jnp.maximum, chunks)
        m_old = m_ref[rows, :]
        m_new = jnp.maximum(m_old, jnp.max(m_tile, axis=-1, keepdims=True))
        alpha = jnp.exp2(m_old - m_new)
        pr = jnp.concatenate([jnp.exp2(ch - m_new).astype(BF16) for ch in chunks], axis=1)
        pv = jnp.dot(pr, v_ref[0, 0, :nk, :], preferred_element_type=F32)
        acc_ref[rows, :] = jnp.concatenate([alpha, alpha], axis=1) * acc_ref[rows, :] + pv
        m_ref[rows, :] = m_new

    def tile_update(masked):
        for r0 in range(0, tq, row_block):
            update(qa_ref, m1_ref, acc1_ref, r0, masked)
            update(qb_ref, m2_ref, acc2_ref, r0, masked)

    @pl.when(ki < qi)
    def _():
        tile_update(False)

    @pl.when(ki == qi)
    def _():
        tile_update(True)
        lam = (jnp.exp(jnp.sum(lq1_ref[...] * lk1_ref[...], axis=-1, keepdims=True))
               - jnp.exp(jnp.sum(lq2_ref[...] * lk2_ref[...], axis=-1, keepdims=True)) + lambda_init)
        o = (acc1_ref[:, :LANES] / acc1_ref[:, LANES:]
             - lam * (acc2_ref[:, :LANES] / acc2_ref[:, LANES:]))
        o_ref[0] = _rms(o, sub_ref[...], DA_SUBLN_EPS) * (1.0 - lambda_init)


def _attn(qa, qb, k, vaug, lq1, lk1, lq2, lk2, sub, lambda_init, tile):
    b, nh, s, dv2 = vaug.shape
    nq = s // tile
    pairs = [(q, kk) for q in range(nq) for kk in range(q + 1)]
    qi = jnp.asarray(np.array([pq for pq, _ in pairs], np.int32))
    ki = jnp.asarray(np.array([pk for _, pk in pairs], np.int32))
    qspec = pl.BlockSpec((1, 1, tile, LANES), lambda bi, h, t, qi, ki: (bi, h, qi[t], 0))
    kspec = pl.BlockSpec((1, 1, tile, LANES), lambda bi, h, t, qi, ki: (bi, h, ki[t], 0))
    vspec = pl.BlockSpec((1, 1, tile, dv2), lambda bi, h, t, qi, ki: (bi, h, ki[t], 0))
    small = lambda arr: pl.BlockSpec(arr.shape, lambda bi, h, t, qi, ki: (0,) * arr.ndim)
    grid_spec = pltpu.PrefetchScalarGridSpec(
        num_scalar_prefetch=2,
        grid=(b, nh, len(pairs)),
        in_specs=[qspec, qspec, kspec, vspec, small(lq1), small(lk1), small(lq2), small(lk2), small(sub)],
        out_specs=pl.BlockSpec((1, tile, LANES), lambda bi, h, t, qi, ki: (bi, qi[t], h)),
        scratch_shapes=[pltpu.VMEM((tile, LANES), F32), pltpu.VMEM((tile, dv2), F32)] * 2,
    )
    return pl.pallas_call(
        functools.partial(_attn_kernel, lambda_init=lambda_init, row_block=min(ATTN_ROW_BLOCK, tile)),
        grid_spec=grid_spec,
        out_shape=jax.ShapeDtypeStruct((b, s, nh * LANES), F32),
        compiler_params=_cparams(("parallel", "parallel", "arbitrary")),
        name="diff_attn",
    )(qi, ki, qa, qb, k, vaug, lq1, lk1, lq2, lk2, sub)


def _tail_kernel(h_ref, o_ref, g_ref, p_ref, wo_ref, pn_ref, pg_ref, pp_ref, fn_ref, out_ref):
    gate = g_ref[0]
    z = o_ref[0] * (gate * _sigmoid(gate))
    h = h_ref[0] + _bdot(z, wo_ref[...])
    h = _ple(h, p_ref[0], pn_ref[...], pg_ref[...], pp_ref[...])
    out_ref[0] = _rms(h, fn_ref[...], NORM_EPS)


def _tail(h, o, g, p, wo, pn, pg, pp, fn, tm):
    b, s, d = h.shape
    pd = p.shape[2]
    full = lambda arr: pl.BlockSpec(arr.shape, lambda bi, i: (0,) * arr.ndim)
    row = lambda w: pl.BlockSpec((1, tm, w), lambda bi, i: (bi, i, 0))
    return pl.pallas_call(
        _tail_kernel,
        grid=(b, s // tm),
        in_specs=[row(d), row(d), row(d), row(pd), full(wo), full(pn), full(pg), full(pp), full(fn)],
        out_specs=row(d),
        out_shape=jax.ShapeDtypeStruct((b, s, d), F32),
        compiler_params=_cparams(("parallel", "parallel")),
        name="tail",
    )(h, o, g, p, wo, pn, pg, pp, fn)


def _rope_tables(s):
    dk = DA_QK_DIM
    pos = jnp.arange(s, dtype=F32)
    inv_freq = 1.0 / (ROPE_THETA ** (jnp.arange(0, dk, 2, dtype=F32) / dk))
    ang = pos[:, None] * inv_freq[None, :]
    ang = jnp.concatenate([ang, ang], axis=-1)
    cos = jnp.cos(ang)
    sin = jnp.sin(ang)
    sign = jnp.where(jnp.arange(dk) < dk // 2, -1.0, 1.0).astype(F32)
    sin = sin * sign[None, :]
    return jnp.concatenate([cos, cos], axis=-1), jnp.concatenate([sin, sin], axis=-1)


def kernel(x, p, rw_norm, rw_mu, rw_w_in, rw_w0, rw_w1, rw_w2, rw_a0, rw_a1, rw_a2, rw_k_k, rw_k_a, rw_r_k, rw_ln_w, rw_ln_b, rw_w_out, da_norm, da_w_in, da_lq1, da_lk1, da_lq2, da_lk2, da_subln, da_w_out, pe_norm, pe_w_gate, pe_w_proj, final_norm):
    b, s, d = x.shape
    assert p.shape[0] == 2 and rw_norm.shape[0] == 1 and da_norm.shape[0] == 1
    tm = min(256, s)
    tile = min(ATTN_TILE, s)
    row = lambda vec: vec.reshape(1, -1)
    bf = lambda w: w.astype(BF16)

    r, k, v, gate, lw, a = _rwkv_in(
        x, row(rw_norm[0]), rw_mu[0], bf(rw_w_in[0]), row(rw_w0[0]), bf(rw_w1[0]), bf(rw_w2[0]),
        row(rw_a0[0]), bf(rw_a1[0]), bf(rw_a2[0]), tm)
    y = _rwkv_scan(r, k, v, lw, a, row(rw_k_k[0]), row(rw_k_a[0]), row(rw_r_k[0]), row(rw_ln_w[0]),
                   row(rw_ln_b[0]))
    cos, sin = _rope_tables(s)
    h1, qa, qb, kq, vq, gate2 = _mid(
        x, y, gate, p[0], bf(rw_w_out[0]), row(pe_norm[0]), bf(pe_w_gate[0]), bf(pe_w_proj[0]),
        row(da_norm[0]), bf(da_w_in[0]), cos, sin, tm)
    lambda_init = 0.8 - 0.6 * math.exp(-0.3 * 1)
    o = _attn(qa, qb, kq, vq, row(da_lq1[0]), row(da_lk1[0]), row(da_lq2[0]), row(da_lk2[0]),
              row(da_subln[0]), lambda_init, tile)
    return _tail(h1, o, gate2, p[1], bf(da_w_out[0]), row(pe_norm[1]), bf(pe_w_gate[1]),
                 bf(pe_w_proj[1]), row(final_norm), tm)
```

```python
import functools
import math

import jax
import jax.numpy as jnp
import numpy as np
from jax import lax
from jax.experimental import pallas as pl
from jax.experimental.pallas import tpu as pltpu

F32 = jnp.float32
BF16 = jnp.bfloat16

LANES = 128
MXU_WIDTH = 256
NORM_EPS = 1e-6
GN_EPS = 64e-5
DA_SUBLN_EPS = 1e-5
ROPE_THETA = 10000.0
RW_HEAD = 64
DA_QK_DIM = 64
DA_HEADS = 8
LORA = 64
SCAN_CHUNK = 64
ATTN_TILE = 1024
ATTN_ROW_BLOCK = 256
ATTN_HEADS_PER_STEP = 4
VMEM_LIMIT = 56 * 1024 * 1024
EXP_NEG_HALF = math.exp(-0.5)


def _cparams(sem):
    return pltpu.CompilerParams(dimension_semantics=sem, vmem_limit_bytes=VMEM_LIMIT)


def _rms(x, g, eps):
    return x * lax.rsqrt(jnp.mean(x * x, axis=-1, keepdims=True) + eps) * g


def _sigmoid(x):
    return 1.0 / (1.0 + jnp.exp(-x))


def _bdot(a, b):
    return jnp.dot(a.astype(BF16), b.astype(BF16), preferred_element_type=F32)


def _rwkv_in_kernel(x_ref, xp_ref, nrm_ref, mu_ref, win_ref, w0_ref, w1_ref, w2_ref,
                    a0_ref, a1_ref, a2_ref,
                    r_ref, k_ref, v_ref, g_ref, lw_ref, a_ref):
    i = pl.program_id(1)
    tm = x_ref.shape[1]
    c = r_ref.shape[2]
    nrm = nrm_ref[...]
    hn = _rms(x_ref[0], nrm, NORM_EPS)
    prev_row = _rms(xp_ref[0], nrm, NORM_EPS)[7:8, :]
    prev_row = jnp.where(i == 0, 0.0, prev_row)
    rows = lax.broadcasted_iota(jnp.int32, hn.shape, 0)
    shifted = jnp.where(rows == 0, prev_row, pltpu.roll(hn, 1, 0))
    dx = shifted - hn
    mu = mu_ref[...]

    def lerp(j):
        return (hn + dx * mu[j:j + 1, :]).astype(BF16)

    r_ref[0] = jnp.dot(lerp(0), win_ref[:, 0 * c:1 * c], preferred_element_type=F32)
    k_ref[0] = jnp.dot(lerp(1), win_ref[:, 1 * c:2 * c], preferred_element_type=F32)
    v_ref[0] = jnp.dot(lerp(2), win_ref[:, 2 * c:3 * c], preferred_element_type=F32)
    g_ref[0] = jnp.dot(lerp(3), win_ref[:, 3 * c:4 * c], preferred_element_type=F32)
    tw = jnp.tanh(jnp.dot(lerp(4), w1_ref[...], preferred_element_type=F32))
    zw = w0_ref[...] + _bdot(tw, w2_ref[...])
    lw_ref[0] = -EXP_NEG_HALF * _sigmoid(zw)
    ta = jnp.dot(lerp(5), a1_ref[...], preferred_element_type=F32)
    a_ref[0] = _sigmoid(a0_ref[...] + _bdot(ta, a2_ref[...]))


def _rwkv_in(x, nrm, mu, win, w0, w1, w2, a0, a1, a2, tm):
    b, s, d = x.shape
    c = win.shape[1] // 4
    full = lambda arr: pl.BlockSpec(arr.shape, lambda bi, i: (0,) * arr.ndim)
    out_sds = jax.ShapeDtypeStruct((b, s, c), F32)
    out_spec = pl.BlockSpec((1, tm, c), lambda bi, i: (bi, i, 0))
    return pl.pallas_call(
        _rwkv_in_kernel,
        grid=(b, s // tm),
        in_specs=[
            pl.BlockSpec((1, tm, d), lambda bi, i: (bi, i, 0)),
            pl.BlockSpec((1, 8, d), lambda bi, i: (bi, jnp.maximum(i * (tm // 8) - 1, 0), 0)),
            full(nrm), full(mu), full(win), full(w0), full(w1), full(w2), full(a0), full(a1), full(a2),
        ],
        out_specs=[out_spec] * 6,
        out_shape=[out_sds] * 6,
        compiler_params=_cparams(("parallel", "parallel")),
        name="rwkv_in",
    )(x, x, nrm, mu, win, w0, w1, w2, a0, a1, a2)


def _hi_lo(x):
    hi = x.astype(BF16)
    lo = (x - hi.astype(F32)).astype(BF16)
    return hi, lo


def _cat2(xs, ys, axis):
    return (jnp.concatenate([xs[0], ys[0]], axis=axis), jnp.concatenate([xs[1], ys[1]], axis=axis))


def _dot3(a, b, dims):
    dn = (((dims[0],), (dims[1],)), ((), ()))
    free = 1 - dims[1]
    n = b[0].shape[free]
    if n <= MXU_WIDTH // 2:
        lhs = jnp.concatenate([a[0], a[1]], axis=dims[0])
        top = jnp.concatenate([b[0], b[1]], axis=free)
        bot = jnp.concatenate([b[0], jnp.zeros_like(b[0])], axis=free)
        out = lax.dot_general(lhs, jnp.concatenate([top, bot], axis=dims[1]), dn,
                              preferred_element_type=F32)
        return out[:, :n] + out[:, n:]
    lhs = jnp.concatenate([a[0], a[1], a[0]], axis=dims[0])
    rhs = jnp.concatenate([b[0], b[0], b[1]], axis=dims[1])
    return lax.dot_general(lhs, rhs, dn, preferred_element_type=F32)


def _mm(a, b):
    return _dot3(a, b, (1, 0))


def _mm_nt(a, b):
    return _dot3(a, b, (1, 1))


def _mm_tn(a, b):
    return _dot3(a, b, (0, 0))


def _scan_kernel(r_ref, k_ref, v_ref, lw_ref, a_ref, kk_ref, ka_ref, rk_ref, lnw_ref, lnb_ref,
                 y_ref, st_ref):
    nb, t, c = r_ref.shape
    ppb = c // LANES
    npairs = nb * ppb
    t2 = 2 * t

    @pl.when(pl.program_id(0) == 0)
    def _():
        st_ref[...] = jnp.zeros_like(st_ref)

    wide = lambda ref: jnp.concatenate([ref[b] for b in range(nb)], axis=1)
    tiled = lambda ref: jnp.concatenate([ref[...]] * nb, axis=1)
    r = wide(r_ref)
    kr = wide(k_ref)
    v = wide(v_ref)
    lw = wide(lw_ref)
    a = wide(a_ref)
    lnw = tiled(lnw_ref)
    lnb = tiled(lnb_ref)

    row_t = lax.broadcasted_iota(jnp.int32, (t, t), 0)
    col_t = lax.broadcasted_iota(jnp.int32, (t, t), 1)
    tri = (row_t >= col_t).astype(BF16)
    lw_h = lw.astype(BF16)
    lw_r = lw - lw_h.astype(F32)
    lw_m = lw_r.astype(BF16)
    lw_l = (lw_r - lw_m.astype(F32)).astype(BF16)
    cum = (jnp.dot(tri, lw_h, preferred_element_type=F32) + jnp.dot(tri, lw_m, preferred_element_type=F32)
           + jnp.dot(tri, lw_l, preferred_element_type=F32))
    cum_last = cum[t - 1:t, :]
    e_cum = jnp.exp(cum)
    e_prev = jnp.exp(cum - lw)
    e_inv = jnp.exp(-cum)
    e_tail = jnp.exp(cum_last - cum)
    e_last = jnp.exp(cum_last)
    kku = kr * tiled(kk_ref)
    km = kr * (1.0 + (a - 1.0) * tiled(ka_ref))
    rkr = r * km * tiled(rk_ref)
    r_dec = r * e_cum

    lane = lax.broadcasted_iota(jnp.int32, (1, LANES), 1)
    first = lane < RW_HEAD
    rl = lax.broadcasted_iota(jnp.int32, (LANES, LANES), 0)
    cl = lax.broadcasted_iota(jnp.int32, (LANES, LANES), 1)
    same_head = (rl >> 6) == (cl >> 6)
    head_ones = same_head.astype(BF16)
    head_ones2 = jnp.concatenate([head_ones, head_ones], axis=0)

    def head_sum(x):
        xh, xl = _hi_lo(x)
        return jnp.dot(jnp.concatenate([xh, xl], axis=1), head_ones2, preferred_element_type=F32)

    r2 = lax.broadcasted_iota(jnp.int32, (t2, t2), 0)
    c2 = lax.broadcasted_iota(jnp.int32, (t2, t2), 1)
    strict = (r2 & (t - 1)) > (c2 & (t - 1))
    blk16 = (r2 >> 4) == (c2 >> 4)
    blk32 = (r2 >> 5) == (c2 >> 5)
    eye = (r2 == c2).astype(F32)
    r1 = lax.broadcasted_iota(jnp.int32, (t, t2), 0)
    c1 = lax.broadcasted_iota(jnp.int32, (t, t2), 1)
    incl = r1 >= (c1 & (t - 1))

    zero = jnp.zeros((), BF16)

    def stack(x):
        return tuple(jnp.concatenate([jnp.where(first, h, zero), jnp.where(first, zero, h)], axis=0)
                     for h in x)

    prs = range(npairs)
    sls = [slice(p * LANES, (p + 1) * LANES) for p in prs]
    each = lambda f, *xs: [f(*args) for args in zip(*xs)]

    sums = [head_sum(jnp.concatenate([kku[:, sl] * kku[:, sl], rkr[:, sl]], axis=0)) for sl in sls]
    kk = [kku[:, sl] / jnp.maximum(jnp.sqrt(sm[:t]), 1e-12) for sl, sm in zip(sls, sums)]
    bonus = [sm[t:] * v[:, sl] for sl, sm in zip(sls, sums)]
    bb = [kk_p * a[:, sl] for sl, kk_p in zip(sls, kk)]
    v_p = [_hi_lo(v[:, sl]) for sl in sls]
    r_p = [_hi_lo(r_dec[:, sl]) for sl in sls]
    kk_st = [stack(_hi_lo(kk_p * e_prev[:, sl])) for sl, kk_p in zip(sls, kk)]
    v_st = each(stack, v_p)
    kb_hat = [_cat2(stack(_hi_lo(km[:, sl] * e_inv[:, sl])), stack(_hi_lo(bb_p * e_inv[:, sl])), 0)
              for sl, bb_p in zip(sls, bb)]
    kb_bar = [_hi_lo(jnp.concatenate([km[:, sl] * e_tail[:, sl], -(bb_p * e_tail[:, sl])], axis=0))
              for sl, bb_p in zip(sls, bb)]

    g = [_mm_nt(_cat2(kk_st[p], r_p[p], 0), kb_hat[p]) for p in prs]
    a_k = [_hi_lo(jnp.concatenate([jnp.where(strict, g_p[:t2, :t2], 0.0),
                                   jnp.where(incl, g_p[t2:, :t2], 0.0)], axis=0)) for g_p in g]
    a_kb = [jnp.where(strict, g_p[:t2, t2:], 0.0) for g_p in g]
    a_rb = [_hi_lo(jnp.where(incl, g_p[t2:, t2:], 0.0)) for g_p in g]

    n1 = [jnp.where(blk16, x, 0.0) for x in a_kb]
    n1s = each(_hi_lo, n1)
    ns = [_hi_lo(_mm(x, x)) for x in n1s]
    m = [eye - x for x in n1]
    for _ in range(2):
        prod = [_mm(_cat2(_hi_lo(m_p), n_p, 0), n_p) for m_p, n_p in zip(m, ns)]
        m = [m_p + x[:t2] for m_p, x in zip(m, prod)]
        ns = [_hi_lo(x[t2:]) for x in prod]
    m = [m_p + _mm(_hi_lo(m_p), n_p) for m_p, n_p in zip(m, ns)]
    for hb, sel in ((16, blk32 & jnp.logical_not(blk16)), (32, jnp.logical_not(blk32))):
        lower = lambda x: jnp.concatenate([x[i:i + hb] for i in range(hb, t2, 2 * hb)], axis=0)
        ms = each(_hi_lo, m)
        es = [_hi_lo(jnp.where(sel, x, 0.0)) for x in a_kb]
        me = [_hi_lo(_mm(_hi_lo(lower(m_p)), e_p)) for m_p, e_p in zip(m, es)]
        mem = [_mm(me_p, ms_p) for me_p, ms_p in zip(me, ms)]
        pad = jnp.zeros((hb, t2), F32)
        spread = lambda x: jnp.concatenate(
            [blk for j in range(0, t, hb) for blk in (pad, x[j:j + hb])], axis=0)
        m = [m_p - spread(x) for m_p, x in zip(m, mem)]
    ms = each(_hi_lo, m)

    av = [_mm(a_k[p], v_st[p]) for p in prs]
    w = [_mm(ms[p], _cat2(_hi_lo(av[p][:t2]), kk_st[p], 1)) for p in prs]
    ht = [_hi_lo(st_ref[p]) for p in prs]
    sh = [_mm_nt(_cat2(_hi_lo(w[p][:, LANES:]), r_p[p], 0), ht[p]) for p in prs]
    u_st = [w[p][:, :LANES] + sh[p][:t2] for p in prs]
    u = [x[:t] + x[t:] for x in u_st]
    upd = [_mm_tn(_cat2(v_p[p], _hi_lo(u[p]), 0), kb_bar[p]) for p in prs]
    for p in prs:
        st_ref[p] = st_ref[p] * e_last[:, sls[p]] + jnp.where(same_head, upd[p], 0.0)
    y = [sh[p][t2:] + av[p][t2:] - _mm(a_rb[p], _hi_lo(u_st[p])) for p in prs]

    mean = [head_sum(x) * (1.0 / RW_HEAD) for x in y]
    d = [x - mu for x, mu in zip(y, mean)]
    var = [head_sum(x * x) * (1.0 / RW_HEAD) for x in d]
    for p in prs:
        sl = sls[p]
        yn = d[p] * lax.rsqrt(var[p] + GN_EPS) * lnw[:, sl] + lnb[:, sl]
        y_ref[p // ppb, :, sls[p % ppb]] = yn + bonus[p]


def _rwkv_scan(r, k, v, lw, a, k_k, k_a, r_k, ln_w, ln_b):
    b, s, c = r.shape
    t = SCAN_CHUNK
    seq = pl.BlockSpec((b, t, c), lambda i: (0, i, 0))
    par = pl.BlockSpec((1, c), lambda i: (0, 0))
    return pl.pallas_call(
        _scan_kernel,
        grid=(s // t,),
        in_specs=[seq] * 5 + [par] * 5,
        out_specs=seq,
        out_shape=jax.ShapeDtypeStruct((b, s, c), F32),
        scratch_shapes=[pltpu.VMEM((b * c // LANES, LANES, LANES), F32)],
        compiler_params=_cparams(("arbitrary",)),
        name="rwkv_scan",
    )(r, k, v, lw, a, k_k, k_a, r_k, ln_w, ln_b)


def _ple(h, p, nrm, wg, wp):
    g = _sigmoid(_bdot(_rms(h, nrm, NORM_EPS), wg))
    return h + g * _bdot(p, wp)


def _mid_kernel(x_ref, y_ref, g_ref, p_ref, wo_ref, pn_ref, pg_ref, pp_ref, dn_ref, din_ref,
                cos_ref, sin_ref,
                h_ref, qa_ref, qb_ref, k_ref, v_ref, go_ref):
    nh = qa_ref.shape[1]
    gate = g_ref[0]
    z = y_ref[0] * (gate * _sigmoid(gate))
    h = x_ref[0] + _bdot(z, wo_ref[...])
    h = _ple(h, p_ref[0], pn_ref[...], pg_ref[...], pp_ref[...])
    h_ref[0] = h
    proj = _bdot(_rms(h, dn_ref[...], NORM_EPS), din_ref[...])
    cw = nh * LANES
    cos = cos_ref[...]
    sin = sin_ref[...]
    lane = lax.broadcasted_iota(jnp.int32, (1, LANES), 1)
    low = (lane & (DA_QK_DIM - 1)) < (DA_QK_DIM // 2)
    first = lane < DA_QK_DIM
    scale = DA_QK_DIM ** -0.5 * math.log2(math.e)
    ones = jnp.ones((proj.shape[0], LANES), BF16)

    def rope(xh):
        rot = jnp.where(low, pltpu.roll(xh, LANES - DA_QK_DIM // 2, 1), pltpu.roll(xh, DA_QK_DIM // 2, 1))
        return xh * cos + rot * sin

    for hd in range(nh):
        sl = slice(hd * LANES, (hd + 1) * LANES)
        q = rope(proj[:, sl]) * scale
        qa_ref[0, hd] = jnp.where(first, q, 0.0).astype(BF16)
        qb_ref[0, hd] = jnp.where(first, 0.0, q).astype(BF16)
        k_ref[0, hd] = rope(proj[:, cw + hd * LANES:cw + (hd + 1) * LANES]).astype(BF16)
        v_ref[0, hd, :, :LANES] = proj[:, 2 * cw + hd * LANES:2 * cw + (hd + 1) * LANES].astype(BF16)
        v_ref[0, hd, :, LANES:] = ones
    go_ref[0] = proj[:, 3 * cw:]


def _mid(x, y, g, p, wo, pn, pg, pp, dn, din, cos, sin, tm):
    b, s, d = x.shape
    nh = DA_HEADS
    pd = p.shape[2]
    full = lambda arr: pl.BlockSpec(arr.shape, lambda bi, i: (0,) * arr.ndim)
    row = lambda w: pl.BlockSpec((1, tm, w), lambda bi, i: (bi, i, 0))
    hd_spec = pl.BlockSpec((1, nh, tm, LANES), lambda bi, i: (bi, 0, i, 0))
    hd_sds = jax.ShapeDtypeStruct((b, nh, s, LANES), BF16)
    va_spec = pl.BlockSpec((1, nh, tm, 2 * LANES), lambda bi, i: (bi, 0, i, 0))
    va_sds = jax.ShapeDtypeStruct((b, nh, s, 2 * LANES), BF16)
    tab = pl.BlockSpec((tm, LANES), lambda bi, i: (i, 0))
    return pl.pallas_call(
        _mid_kernel,
        grid=(b, s // tm),
        in_specs=[row(d), row(d), row(d), row(pd), full(wo), full(pn), full(pg), full(pp), full(dn),
                  full(din), tab, tab],
        out_specs=[row(d), hd_spec, hd_spec, hd_spec, va_spec, row(d)],
        out_shape=[jax.ShapeDtypeStruct((b, s, d), F32), hd_sds, hd_sds, hd_sds, va_sds,
                   jax.ShapeDtypeStruct((b, s, d), F32)],
        compiler_params=_cparams(("parallel", "parallel")),
        name="mid",
    )(x, y, g, p, wo, pn, pg, pp, dn, din, cos, sin)


def _attn_kernel(qi_ref, ki_ref, qa_ref, qb_ref, k_ref, v_ref, lq1_ref, lk1_ref, lq2_ref, lk2_ref,
                 sub_ref, o_ref, m1_ref, acc1_ref, m2_ref, acc2_ref, *, lambda_init, row_block):
    step = pl.program_id(2)
    qi = qi_ref[step]
    ki = ki_ref[step]
    nhs = qa_ref.shape[1]
    tq = qa_ref.shape[2]
    tk = k_ref.shape[2]

    @pl.when(ki == 0)
    def _():
        m1_ref[...] = jnp.full_like(m1_ref, -jnp.inf)
        m2_ref[...] = jnp.full_like(m2_ref, -jnp.inf)
        acc1_ref[...] = jnp.zeros_like(acc1_ref)
        acc2_ref[...] = jnp.zeros_like(acc2_ref)

    def update(q_ref, m_ref, acc_ref, hh, r0, masked):
        rows = pl.ds(r0, row_block)
        nk = min(r0 + row_block, tk) if masked else tk
        s = lax.dot_general(q_ref[0, hh, rows, :], k_ref[0, hh, :nk, :], (((1,), (1,)), ((), ())),
                            preferred_element_type=F32)
        if masked:
            rq = lax.broadcasted_iota(jnp.int32, (row_block, nk), 0) + r0
            ck = lax.broadcasted_iota(jnp.int32, (row_block, nk), 1)
            s = jnp.where(ck <= rq, s, -jnp.inf)
        chunks = [s[:, c * LANES:(c + 1) * LANES] for c in range(nk // LANES)]
        m_tile = functools.reduce(jnp.maximum, chunks)
        m_old = m_ref[hh, rows, :]
        m_new = jnp.maximum(m_old, jnp.max(m_tile, axis=-1, keepdims=True))
        alpha = jnp.exp2(m_old - m_new)
        pr = jnp.concatenate([jnp.exp2(ch - m_new).astype(BF16) for ch in chunks], axis=1)
        pv = jnp.dot(pr, v_ref[0, hh, :nk, :], preferred_element_type=F32)
        acc_ref[hh, rows, :] = jnp.concatenate([alpha, alpha], axis=1) * acc_ref[hh, rows, :] + pv
        m_ref[hh, rows, :] = m_new

    def tile_update(masked):
        for r0 in range(0, tq, row_block):
            for hh in range(nhs):
                update(qa_ref, m1_ref, acc1_ref, hh, r0, masked)
                update(qb_ref, m2_ref, acc2_ref, hh, r0, masked)

    @pl.when(ki < qi)
    def _():
        tile_update(False)

    @pl.when(ki == qi)
    def _():
        tile_update(True)
        lam = (jnp.exp(jnp.sum(lq1_ref[...] * lk1_ref[...], axis=-1, keepdims=True))
               - jnp.exp(jnp.sum(lq2_ref[...] * lk2_ref[...], axis=-1, keepdims=True)) + lambda_init)
        for hh in range(nhs):
            o = (acc1_ref[hh, :, :LANES] / acc1_ref[hh, :, LANES:]
                 - lam * (acc2_ref[hh, :, :LANES] / acc2_ref[hh, :, LANES:]))
            o_ref[0, :, hh * LANES:(hh + 1) * LANES] = (_rms(o, sub_ref[...], DA_SUBLN_EPS)
                                                        * (1.0 - lambda_init))


def _attn(qa, qb, k, vaug, lq1, lk1, lq2, lk2, sub, lambda_init, tile):
    b, nh, s, dv2 = vaug.shape
    nhs = ATTN_HEADS_PER_STEP
    nq = s // tile
    pairs = [(q, kk) for q in range(nq) for kk in range(q + 1)]
    qi = jnp.asarray(np.array([pq for pq, _ in pairs], np.int32))
    ki = jnp.asarray(np.array([pk for _, pk in pairs], np.int32))
    qspec = pl.BlockSpec((1, nhs, tile, LANES), lambda bi, h, t, qi, ki: (bi, h, qi[t], 0))
    kspec = pl.BlockSpec((1, nhs, tile, LANES), lambda bi, h, t, qi, ki: (bi, h, ki[t], 0))
    vspec = pl.BlockSpec((1, nhs, tile, dv2), lambda bi, h, t, qi, ki: (bi, h, ki[t], 0))
    small = lambda arr: pl.BlockSpec(arr.shape, lambda bi, h, t, qi, ki: (0,) * arr.ndim)
    grid_spec = pltpu.PrefetchScalarGridSpec(
        num_scalar_prefetch=2,
        grid=(b, nh // nhs, len(pairs)),
        in_specs=[qspec, qspec, kspec, vspec, small(lq1), small(lk1), small(lq2), small(lk2), small(sub)],
        out_specs=pl.BlockSpec((1, tile, nhs * LANES), lambda bi, h, t, qi, ki: (bi, qi[t], h)),
        scratch_shapes=[pltpu.VMEM((nhs, tile, LANES), F32), pltpu.VMEM((nhs, tile, dv2), F32)] * 2,
    )
    return pl.pallas_call(
        functools.partial(_attn_kernel, lambda_init=lambda_init, row_block=min(ATTN_ROW_BLOCK, tile)),
        grid_spec=grid_spec,
        out_shape=jax.ShapeDtypeStruct((b, s, nh * LANES), F32),
        compiler_params=_cparams(("parallel", "parallel", "arbitrary")),
        name="diff_attn",
    )(qi, ki, qa, qb, k, vaug, lq1, lk1, lq2, lk2, sub)


def _tail_kernel(h_ref, o_ref, g_ref, p_ref, wo_ref, pn_ref, pg_ref, pp_ref, fn_ref, out_ref):
    gate = g_ref[0]
    z = o_ref[0] * (gate * _sigmoid(gate))
    h = h_ref[0] + _bdot(z, wo_ref[...])
    h = _ple(h, p_ref[0], pn_ref[...], pg_ref[...], pp_ref[...])
    out_ref[0] = _rms(h, fn_ref[...], NORM_EPS)


def _tail(h, o, g, p, wo, pn, pg, pp, fn, tm):
    b, s, d = h.shape
    pd = p.shape[2]
    full = lambda arr: pl.BlockSpec(arr.shape, lambda bi, i: (0,) * arr.ndim)
    row = lambda w: pl.BlockSpec((1, tm, w), lambda bi, i: (bi, i, 0))
    return pl.pallas_call(
        _tail_kernel,
        grid=(b, s // tm),
        in_specs=[row(d), row(d), row(d), row(pd), full(wo), full(pn), full(pg), full(pp), full(fn)],
        out_specs=row(d),
        out_shape=jax.ShapeDtypeStruct((b, s, d), F32),
        compiler_params=_cparams(("parallel", "parallel")),
        name="tail",
    )(h, o, g, p, wo, pn, pg, pp, fn)


def _rope_tables(s):
    dk = DA_QK_DIM
    pos = jnp.arange(s, dtype=F32)
    inv_freq = 1.0 / (ROPE_THETA ** (jnp.arange(0, dk, 2, dtype=F32) / dk))
    ang = pos[:, None] * inv_freq[None, :]
    ang = jnp.concatenate([ang, ang], axis=-1)
    cos = jnp.cos(ang)
    sin = jnp.sin(ang)
    sign = jnp.where(jnp.arange(dk) < dk // 2, -1.0, 1.0).astype(F32)
    sin = sin * sign[None, :]
    return jnp.concatenate([cos, cos], axis=-1), jnp.concatenate([sin, sin], axis=-1)


def kernel(x, p, rw_norm, rw_mu, rw_w_in, rw_w0, rw_w1, rw_w2, rw_a0, rw_a1, rw_a2, rw_k_k, rw_k_a, rw_r_k, rw_ln_w, rw_ln_b, rw_w_out, da_norm, da_w_in, da_lq1, da_lk1, da_lq2, da_lk2, da_subln, da_w_out, pe_norm, pe_w_gate, pe_w_proj, final_norm):
    b, s, d = x.shape
    assert p.shape[0] == 2 and rw_norm.shape[0] == 1 and da_norm.shape[0] == 1
    tm = min(256, s)
    tile = min(ATTN_TILE, s)
    row = lambda vec: vec.reshape(1, -1)
    bf = lambda w: w.astype(BF16)

    r, k, v, gate, lw, a = _rwkv_in(
        x, row(rw_norm[0]), rw_mu[0], bf(rw_w_in[0]), row(rw_w0[0]), bf(rw_w1[0]), bf(rw_w2[0]),
        row(rw_a0[0]), bf(rw_a1[0]), bf(rw_a2[0]), tm)
    y = _rwkv_scan(r, k, v, lw, a, row(rw_k_k[0]), row(rw_k_a[0]), row(rw_r_k[0]), row(rw_ln_w[0]),
                   row(rw_ln_b[0]))
    cos, sin = _rope_tables(s)
    h1, qa, qb, kq, vq, gate2 = _mid(
        x, y, gate, p[0], bf(rw_w_out[0]), row(pe_norm[0]), bf(pe_w_gate[0]), bf(pe_w_proj[0]),
        row(da_norm[0]), bf(da_w_in[0]), cos, sin, tm)
    lambda_init = 0.8 - 0.6 * math.exp(-0.3 * 1)
    o = _attn(qa, qb, kq, vq, row(da_lq1[0]), row(da_lk1[0]), row(da_lq2[0]), row(da_lk2[0]),
              row(da_subln[0]), lambda_init, tile)
    return _tail(h1, o, gate2, p[1], bf(da_w_out[0]), row(pe_norm[1]), bf(pe_w_gate[1]),
                 bf(pe_w_proj[1]), row(final_norm), tm)
```

```python
import functools
import math

import jax
import jax.numpy as jnp
import numpy as np
from jax import lax
from jax.experimental import pallas as pl
from jax.experimental.pallas import tpu as pltpu

F32 = jnp.float32
BF16 = jnp.bfloat16

LANES = 128
MXU_WIDTH = 256
NORM_EPS = 1e-6
GN_EPS = 64e-5
DA_SUBLN_EPS = 1e-5
ROPE_THETA = 10000.0
RW_HEAD = 64
DA_QK_DIM = 64
DA_HEADS = 8
LORA = 64
SCAN_CHUNK = 64
ATTN_TILE = 1024
ATTN_Q_BLOCK = 256
ATTN_K_BLOCK = 512
ATTN_AHEAD_SCORES = 3
ATTN_AHEAD_SOFTMAX = 1
ATTN_ONES_ROWS = 16
ATTN_HEADS_PER_STEP = 4
VMEM_LIMIT = 56 * 1024 * 1024
EXP_NEG_HALF = math.exp(-0.5)


def _cparams(sem):
    return pltpu.CompilerParams(dimension_semantics=sem, vmem_limit_bytes=VMEM_LIMIT)


def _rms(x, g, eps):
    return x * lax.rsqrt(jnp.mean(x * x, axis=-1, keepdims=True) + eps) * g


def _sigmoid(x):
    return 1.0 / (1.0 + jnp.exp(-x))


def _bdot(a, b):
    return jnp.dot(a.astype(BF16), b.astype(BF16), preferred_element_type=F32)


def _rwkv_in_kernel(x_ref, xp_ref, nrm_ref, mu_ref, win_ref, w0_ref, w1_ref, w2_ref,
                    a0_ref, a1_ref, a2_ref,
                    r_ref, k_ref, v_ref, g_ref, lw_ref, a_ref):
    i = pl.program_id(1)
    tm = x_ref.shape[1]
    c = r_ref.shape[2]
    nrm = nrm_ref[...]
    hn = _rms(x_ref[0], nrm, NORM_EPS)
    prev_row = _rms(xp_ref[0], nrm, NORM_EPS)[7:8, :]
    prev_row = jnp.where(i == 0, 0.0, prev_row)
    rows = lax.broadcasted_iota(jnp.int32, hn.shape, 0)
    shifted = jnp.where(rows == 0, prev_row, pltpu.roll(hn, 1, 0))
    dx = shifted - hn
    mu = mu_ref[...]

    def lerp(j):
        return (hn + dx * mu[j:j + 1, :]).astype(BF16)

    r_ref[0] = jnp.dot(lerp(0), win_ref[:, 0 * c:1 * c], preferred_element_type=F32)
    k_ref[0] = jnp.dot(lerp(1), win_ref[:, 1 * c:2 * c], preferred_element_type=F32)
    v_ref[0] = jnp.dot(lerp(2), win_ref[:, 2 * c:3 * c], preferred_element_type=F32)
    g_ref[0] = jnp.dot(lerp(3), win_ref[:, 3 * c:4 * c], preferred_element_type=F32)
    tw = jnp.tanh(jnp.dot(lerp(4), w1_ref[...], preferred_element_type=F32))
    zw = w0_ref[...] + _bdot(tw, w2_ref[...])
    lw_ref[0] = -EXP_NEG_HALF * _sigmoid(zw)
    ta = jnp.dot(lerp(5), a1_ref[...], preferred_element_type=F32)
    a_ref[0] = _sigmoid(a0_ref[...] + _bdot(ta, a2_ref[...]))


def _rwkv_in(x, nrm, mu, win, w0, w1, w2, a0, a1, a2, tm):
    b, s, d = x.shape
    c = win.shape[1] // 4
    full = lambda arr: pl.BlockSpec(arr.shape, lambda bi, i: (0,) * arr.ndim)
    out_sds = jax.ShapeDtypeStruct((b, s, c), F32)
    out_spec = pl.BlockSpec((1, tm, c), lambda bi, i: (bi, i, 0))
    return pl.pallas_call(
        _rwkv_in_kernel,
        grid=(b, s // tm),
        in_specs=[
            pl.BlockSpec((1, tm, d), lambda bi, i: (bi, i, 0)),
            pl.BlockSpec((1, 8, d), lambda bi, i: (bi, jnp.maximum(i * (tm // 8) - 1, 0), 0)),
            full(nrm), full(mu), full(win), full(w0), full(w1), full(w2), full(a0), full(a1), full(a2),
        ],
        out_specs=[out_spec] * 6,
        out_shape=[out_sds] * 6,
        compiler_params=_cparams(("parallel", "parallel")),
        name="rwkv_in",
    )(x, x, nrm, mu, win, w0, w1, w2, a0, a1, a2)


def _hi_lo(x):
    hi = x.astype(BF16)
    lo = (x - hi.astype(F32)).astype(BF16)
    return hi, lo


def _cat2(xs, ys, axis):
    return (jnp.concatenate([xs[0], ys[0]], axis=axis), jnp.concatenate([xs[1], ys[1]], axis=axis))


def _dot3(a, b, dims):
    dn = (((dims[0],), (dims[1],)), ((), ()))
    free = 1 - dims[1]
    n = b[0].shape[free]
    if n <= MXU_WIDTH // 2:
        lhs = jnp.concatenate([a[0], a[1]], axis=dims[0])
        top = jnp.concatenate([b[0], b[1]], axis=free)
        bot = jnp.concatenate([b[0], jnp.zeros_like(b[0])], axis=free)
        out = lax.dot_general(lhs, jnp.concatenate([top, bot], axis=dims[1]), dn,
                              preferred_element_type=F32)
        return out[:, :n] + out[:, n:]
    lhs = jnp.concatenate([a[0], a[1], a[0]], axis=dims[0])
    rhs = jnp.concatenate([b[0], b[0], b[1]], axis=dims[1])
    return lax.dot_general(lhs, rhs, dn, preferred_element_type=F32)


def _mm(a, b):
    return _dot3(a, b, (1, 0))


def _mm_nt(a, b):
    return _dot3(a, b, (1, 1))


def _mm_tn(a, b):
    return _dot3(a, b, (0, 0))


def _scan_kernel(r_ref, k_ref, v_ref, lw_ref, a_ref, kk_ref, ka_ref, rk_ref, lnw_ref, lnb_ref,
                 y_ref, st_ref):
    nb, t, c = r_ref.shape
    ppb = c // LANES
    npairs = nb * ppb
    t2 = 2 * t

    @pl.when(pl.program_id(0) == 0)
    def _():
        st_ref[...] = jnp.zeros_like(st_ref)

    wide = lambda ref: jnp.concatenate([ref[b] for b in range(nb)], axis=1)
    tiled = lambda ref: jnp.concatenate([ref[...]] * nb, axis=1)
    r = wide(r_ref)
    kr = wide(k_ref)
    v = wide(v_ref)
    lw = wide(lw_ref)
    a = wide(a_ref)
    lnw = tiled(lnw_ref)
    lnb = tiled(lnb_ref)

    row_t = lax.broadcasted_iota(jnp.int32, (t, t), 0)
    col_t = lax.broadcasted_iota(jnp.int32, (t, t), 1)
    tri = (row_t >= col_t).astype(BF16)
    lw_h = lw.astype(BF16)
    lw_r = lw - lw_h.astype(F32)
    lw_m = lw_r.astype(BF16)
    lw_l = (lw_r - lw_m.astype(F32)).astype(BF16)
    cum = (jnp.dot(tri, lw_h, preferred_element_type=F32) + jnp.dot(tri, lw_m, preferred_element_type=F32)
           + jnp.dot(tri, lw_l, preferred_element_type=F32))
    cum_last = cum[t - 1:t, :]
    e_cum = jnp.exp(cum)
    e_prev = jnp.exp(cum - lw)
    e_inv = jnp.exp(-cum)
    e_tail = jnp.exp(cum_last - cum)
    e_last = jnp.exp(cum_last)
    kku = kr * tiled(kk_ref)
    km = kr * (1.0 + (a - 1.0) * tiled(ka_ref))
    rkr = r * km * tiled(rk_ref)
    r_dec = r * e_cum

    lane = lax.broadcasted_iota(jnp.int32, (1, LANES), 1)
    first = lane < RW_HEAD
    rl = lax.broadcasted_iota(jnp.int32, (LANES, LANES), 0)
    cl = lax.broadcasted_iota(jnp.int32, (LANES, LANES), 1)
    same_head = (rl >> 6) == (cl >> 6)
    head_ones = same_head.astype(BF16)
    head_ones2 = jnp.concatenate([head_ones, head_ones], axis=0)

    def head_sum(x):
        xh, xl = _hi_lo(x)
        return jnp.dot(jnp.concatenate([xh, xl], axis=1), head_ones2, preferred_element_type=F32)

    r2 = lax.broadcasted_iota(jnp.int32, (t2, t2), 0)
    c2 = lax.broadcasted_iota(jnp.int32, (t2, t2), 1)
    strict = (r2 & (t - 1)) > (c2 & (t - 1))
    blk16 = (r2 >> 4) == (c2 >> 4)
    blk32 = (r2 >> 5) == (c2 >> 5)
    eye = (r2 == c2).astype(F32)
    r1 = lax.broadcasted_iota(jnp.int32, (t, t2), 0)
    c1 = lax.broadcasted_iota(jnp.int32, (t, t2), 1)
    incl = r1 >= (c1 & (t - 1))

    zero = jnp.zeros((), BF16)

    def stack(x):
        return tuple(jnp.concatenate([jnp.where(first, h, zero), jnp.where(first, zero, h)], axis=0)
                     for h in x)

    prs = range(npairs)
    sls = [slice(p * LANES, (p + 1) * LANES) for p in prs]
    each = lambda f, *xs: [f(*args) for args in zip(*xs)]

    sums = [head_sum(jnp.concatenate([kku[:, sl] * kku[:, sl], rkr[:, sl]], axis=0)) for sl in sls]
    kk = [kku[:, sl] / jnp.maximum(jnp.sqrt(sm[:t]), 1e-12) for sl, sm in zip(sls, sums)]
    bonus = [sm[t:] * v[:, sl] for sl, sm in zip(sls, sums)]
    bb = [kk_p * a[:, sl] for sl, kk_p in zip(sls, kk)]
    v_p = [_hi_lo(v[:, sl]) for sl in sls]
    r_p = [_hi_lo(r_dec[:, sl]) for sl in sls]
    kk_st = [stack(_hi_lo(kk_p * e_prev[:, sl])) for sl, kk_p in zip(sls, kk)]
    v_st = each(stack, v_p)
    kb_hat = [_cat2(stack(_hi_lo(km[:, sl] * e_inv[:, sl])), stack(_hi_lo(bb_p * e_inv[:, sl])), 0)
              for sl, bb_p in zip(sls, bb)]
    kb_bar = [_hi_lo(jnp.concatenate([km[:, sl] * e_tail[:, sl], -(bb_p * e_tail[:, sl])], axis=0))
              for sl, bb_p in zip(sls, bb)]

    g = [_mm_nt(_cat2(kk_st[p], r_p[p], 0), kb_hat[p]) for p in prs]
    a_k = [_hi_lo(jnp.concatenate([jnp.where(strict, g_p[:t2, :t2], 0.0),
                                   jnp.where(incl, g_p[t2:, :t2], 0.0)], axis=0)) for g_p in g]
    a_kb = [jnp.where(strict, g_p[:t2, t2:], 0.0) for g_p in g]
    a_rb = [_hi_lo(jnp.where(incl, g_p[t2:, t2:], 0.0)) for g_p in g]

    n1 = [jnp.where(blk16, x, 0.0) for x in a_kb]
    n1s = each(_hi_lo, n1)
    ns = [_hi_lo(_mm(x, x)) for x in n1s]
    m = [eye - x for x in n1]
    for _ in range(2):
        prod = [_mm(_cat2(_hi_lo(m_p), n_p, 0), n_p) for m_p, n_p in zip(m, ns)]
        m = [m_p + x[:t2] for m_p, x in zip(m, prod)]
        ns = [_hi_lo(x[t2:]) for x in prod]
    m = [m_p + _mm(_hi_lo(m_p), n_p) for m_p, n_p in zip(m, ns)]
    for hb, sel in ((16, blk32 & jnp.logical_not(blk16)), (32, jnp.logical_not(blk32))):
        lower = lambda x: jnp.concatenate([x[i:i + hb] for i in range(hb, t2, 2 * hb)], axis=0)
        ms = each(_hi_lo, m)
        es = [_hi_lo(jnp.where(sel, x, 0.0)) for x in a_kb]
        me = [_hi_lo(_mm(_hi_lo(lower(m_p)), e_p)) for m_p, e_p in zip(m, es)]
        mem = [_mm(me_p, ms_p) for me_p, ms_p in zip(me, ms)]
        pad = jnp.zeros((hb, t2), F32)
        spread = lambda x: jnp.concatenate(
            [blk for j in range(0, t, hb) for blk in (pad, x[j:j + hb])], axis=0)
        m = [m_p - spread(x) for m_p, x in zip(m, mem)]
    ms = each(_hi_lo, m)

    av = [_mm(a_k[p], v_st[p]) for p in prs]
    w = [_mm(ms[p], _cat2(_hi_lo(av[p][:t2]), kk_st[p], 1)) for p in prs]
    ht = [_hi_lo(st_ref[p]) for p in prs]
    sh = [_mm_nt(_cat2(_hi_lo(w[p][:, LANES:]), r_p[p], 0), ht[p]) for p in prs]
    u_st = [w[p][:, :LANES] + sh[p][:t2] for p in prs]
    u = [x[:t] + x[t:] for x in u_st]
    upd = [_mm_tn(_cat2(v_p[p], _hi_lo(u[p]), 0), kb_bar[p]) for p in prs]
    for p in prs:
        st_ref[p] = st_ref[p] * e_last[:, sls[p]] + jnp.where(same_head, upd[p], 0.0)
    y = [sh[p][t2:] + av[p][t2:] - _mm(a_rb[p], _hi_lo(u_st[p])) for p in prs]

    mean = [head_sum(x) * (1.0 / RW_HEAD) for x in y]
    d = [x - mu for x, mu in zip(y, mean)]
    var = [head_sum(x * x) * (1.0 / RW_HEAD) for x in d]
    for p in prs:
        sl = sls[p]
        yn = d[p] * lax.rsqrt(var[p] + GN_EPS) * lnw[:, sl] + lnb[:, sl]
        y_ref[p // ppb, :, sls[p % ppb]] = yn + bonus[p]


def _rwkv_scan(r, k, v, lw, a, k_k, k_a, r_k, ln_w, ln_b):
    b, s, c = r.shape
    t = SCAN_CHUNK
    seq = pl.BlockSpec((b, t, c), lambda i: (0, i, 0))
    par = pl.BlockSpec((1, c), lambda i: (0, 0))
    return pl.pallas_call(
        _scan_kernel,
        grid=(s // t,),
        in_specs=[seq] * 5 + [par] * 5,
        out_specs=seq,
        out_shape=jax.ShapeDtypeStruct((b, s, c), F32),
        scratch_shapes=[pltpu.VMEM((b * c // LANES, LANES, LANES), F32)],
        compiler_params=_cparams(("arbitrary",)),
        name="rwkv_scan",
    )(r, k, v, lw, a, k_k, k_a, r_k, ln_w, ln_b)


def _ple(h, p, nrm, wg, wp):
    g = _sigmoid(_bdot(_rms(h, nrm, NORM_EPS), wg))
    return h + g * _bdot(p, wp)


def _mid_kernel(x_ref, y_ref, g_ref, p_ref, wo_ref, pn_ref, pg_ref, pp_ref, dn_ref, din_ref,
                cos_ref, sin_ref,
                h_ref, qa_ref, qb_ref, k_ref, v_ref, go_ref):
    nh = qa_ref.shape[1]
    gate = g_ref[0]
    z = y_ref[0] * (gate * _sigmoid(gate))
    h = x_ref[0] + _bdot(z, wo_ref[...])
    h = _ple(h, p_ref[0], pn_ref[...], pg_ref[...], pp_ref[...])
    h_ref[0] = h
    proj = _bdot(_rms(h, dn_ref[...], NORM_EPS), din_ref[...])
    cw = nh * LANES
    cos = cos_ref[...]
    sin = sin_ref[...]
    lane = lax.broadcasted_iota(jnp.int32, (1, LANES), 1)
    low = (lane & (DA_QK_DIM - 1)) < (DA_QK_DIM // 2)
    first = lane < DA_QK_DIM
    scale = DA_QK_DIM ** -0.5 * math.log2(math.e)
    ones = jnp.ones((ATTN_ONES_ROWS, proj.shape[0]), BF16)

    def rope(xh):
        rot = jnp.where(low, pltpu.roll(xh, LANES - DA_QK_DIM // 2, 1), pltpu.roll(xh, DA_QK_DIM // 2, 1))
        return xh * cos + rot * sin

    for hd in range(nh):
        sl = slice(hd * LANES, (hd + 1) * LANES)
        q = rope(proj[:, sl]) * scale
        qa_ref[0, hd] = jnp.where(first, q, 0.0).astype(BF16)
        qb_ref[0, hd] = jnp.where(first, 0.0, q).astype(BF16)
        k_ref[0, hd] = rope(proj[:, cw + hd * LANES:cw + (hd + 1) * LANES]).astype(BF16)
        v_ref[0, hd, :LANES, :] = proj[:, 2 * cw + hd * LANES:2 * cw + (hd + 1) * LANES].T.astype(BF16)
        v_ref[0, hd, LANES:, :] = ones
    go_ref[0] = proj[:, 3 * cw:]


def _mid(x, y, g, p, wo, pn, pg, pp, dn, din, cos, sin, tm):
    b, s, d = x.shape
    nh = DA_HEADS
    pd = p.shape[2]
    full = lambda arr: pl.BlockSpec(arr.shape, lambda bi, i: (0,) * arr.ndim)
    row = lambda w: pl.BlockSpec((1, tm, w), lambda bi, i: (bi, i, 0))
    hd_spec = pl.BlockSpec((1, nh, tm, LANES), lambda bi, i: (bi, 0, i, 0))
    hd_sds = jax.ShapeDtypeStruct((b, nh, s, LANES), BF16)
    va_spec = pl.BlockSpec((1, nh, LANES + ATTN_ONES_ROWS, tm), lambda bi, i: (bi, 0, 0, i))
    va_sds = jax.ShapeDtypeStruct((b, nh, LANES + ATTN_ONES_ROWS, s), BF16)
    tab = pl.BlockSpec((tm, LANES), lambda bi, i: (i, 0))
    return pl.pallas_call(
        _mid_kernel,
        grid=(b, s // tm),
        in_specs=[row(d), row(d), row(d), row(pd), full(wo), full(pn), full(pg), full(pp), full(dn),
                  full(din), tab, tab],
        out_specs=[row(d), hd_spec, hd_spec, hd_spec, va_spec, row(d)],
        out_shape=[jax.ShapeDtypeStruct((b, s, d), F32), hd_sds, hd_sds, hd_sds, va_sds,
                   jax.ShapeDtypeStruct((b, s, d), F32)],
        compiler_params=_cparams(("parallel", "parallel")),
        name="mid",
    )(x, y, g, p, wo, pn, pg, pp, dn, din, cos, sin)


def _attn_kernel(qi_ref, ki_ref, qa_ref, qb_ref, k_ref, v_ref, lq1_ref, lk1_ref, lq2_ref, lk2_ref,
                 sub_ref, o_ref, m1_ref, acc1_ref, m2_ref, acc2_ref, *, lambda_init, q_block, k_block):
    step = pl.program_id(2)
    qi = qi_ref[step]
    ki = ki_ref[step]
    nhs = qa_ref.shape[1]
    tq = qa_ref.shape[2]
    tk = k_ref.shape[2]

    @pl.when(ki == 0)
    def _():
        m1_ref[...] = jnp.full_like(m1_ref, -jnp.inf)
        m2_ref[...] = jnp.full_like(m2_ref, -jnp.inf)
        acc1_ref[...] = jnp.zeros_like(acc1_ref)
        acc2_ref[...] = jnp.zeros_like(acc2_ref)

    comps = ((qa_ref, m1_ref, acc1_ref), (qb_ref, m2_ref, acc2_ref))

    def scores(job):
        comp, hh, c0, k0 = job
        return lax.dot_general(k_ref[0, hh, pl.ds(k0, k_block), :], comps[comp][0][0, hh, pl.ds(c0, q_block), :],
                               (((1,), (1,)), ((), ())), preferred_element_type=F32)

    def softmax(job, s, masked):
        comp, hh, c0, k0 = job
        m_ref = comps[comp][1]
        cols = pl.ds(c0, q_block)
        if masked and k0 + k_block > c0 + 1:
            rk = lax.broadcasted_iota(jnp.int32, (k_block, q_block), 0) + k0
            cq = lax.broadcasted_iota(jnp.int32, (k_block, q_block), 1) + c0
            s = jnp.where(rk <= cq, s, -jnp.inf)
        m_old = m_ref[hh, :, cols]
        m_new = jnp.maximum(m_old, jnp.max(s, axis=0, keepdims=True))
        m_ref[hh, :, cols] = m_new
        alpha = jnp.exp2(m_old[:1] - m_new[:1])
        return jnp.exp2(s - m_new[:1]).astype(BF16), alpha

    def accumulate(job, pr, alpha):
        comp, hh, c0, k0 = job
        acc_ref = comps[comp][2]
        cols = pl.ds(c0, q_block)
        pv = jnp.dot(v_ref[0, hh, :, pl.ds(k0, k_block)], pr, preferred_element_type=F32)
        acc_ref[hh, :, cols] = alpha * acc_ref[hh, :, cols] + pv

    def tile_update(masked):
        jobs = [(comp, hh, c0, k0)
                for k0 in range(0, tk, k_block) for c0 in range(0, tq, q_block)
                for hh in range(nhs) for comp in range(2)
                if not (masked and k0 >= c0 + q_block)]
        n = len(jobs)
        s, p = {}, {}
        for i in range(-ATTN_AHEAD_SCORES, n):
            js, jp = i + ATTN_AHEAD_SCORES, i + ATTN_AHEAD_SOFTMAX
            if js < n:
                s[js] = scores(jobs[js])
            if 0 <= jp < n:
                p[jp] = softmax(jobs[jp], s.pop(jp), masked)
            if i >= 0:
                accumulate(jobs[i], *p.pop(i))

    @pl.when(ki < qi)
    def _():
        tile_update(False)

    @pl.when(ki == qi)
    def _():
        tile_update(True)
        lam = (jnp.exp(jnp.sum(lq1_ref[...] * lk1_ref[...], axis=-1, keepdims=True))
               - jnp.exp(jnp.sum(lq2_ref[...] * lk2_ref[...], axis=-1, keepdims=True)) + lambda_init)
        for hh in range(nhs):
            ot = (acc1_ref[hh, :LANES, :] / acc1_ref[hh, LANES:LANES + 1, :]
                  - lam * (acc2_ref[hh, :LANES, :] / acc2_ref[hh, LANES:LANES + 1, :]))
            o_ref[0, :, hh * LANES:(hh + 1) * LANES] = (_rms(ot.T, sub_ref[...], DA_SUBLN_EPS)
                                                        * (1.0 - lambda_init))


def _attn(qa, qb, k, vt, lq1, lk1, lq2, lk2, sub, lambda_init, tile):
    b, nh, dvr, s = vt.shape
    nhs = ATTN_HEADS_PER_STEP
    nq = s // tile
    pairs = [(q, kk) for q in range(nq) for kk in range(q + 1)]
    qi = jnp.asarray(np.array([pq for pq, _ in pairs], np.int32))
    ki = jnp.asarray(np.array([pk for _, pk in pairs], np.int32))
    qspec = pl.BlockSpec((1, nhs, tile, LANES), lambda bi, h, t, qi, ki: (bi, h, qi[t], 0))
    kspec = pl.BlockSpec((1, nhs, tile, LANES), lambda bi, h, t, qi, ki: (bi, h, ki[t], 0))
    vspec = pl.BlockSpec((1, nhs, dvr, tile), lambda bi, h, t, qi, ki: (bi, h, 0, ki[t]))
    small = lambda arr: pl.BlockSpec(arr.shape, lambda bi, h, t, qi, ki: (0,) * arr.ndim)
    grid_spec = pltpu.PrefetchScalarGridSpec(
        num_scalar_prefetch=2,
        grid=(b, nh // nhs, len(pairs)),
        in_specs=[qspec, qspec, kspec, vspec, small(lq1), small(lk1), small(lq2), small(lk2), small(sub)],
        out_specs=pl.BlockSpec((1, tile, nhs * LANES), lambda bi, h, t, qi, ki: (bi, qi[t], h)),
        scratch_shapes=[pltpu.VMEM((nhs, 8, tile), F32), pltpu.VMEM((nhs, dvr, tile), F32)] * 2,
    )
    return pl.pallas_call(
        functools.partial(_attn_kernel, lambda_init=lambda_init, q_block=min(ATTN_Q_BLOCK, tile),
                          k_block=min(ATTN_K_BLOCK, tile)),
        grid_spec=grid_spec,
        out_shape=jax.ShapeDtypeStruct((b, s, nh * LANES), F32),
        compiler_params=_cparams(("parallel", "parallel", "arbitrary")),
        name="diff_attn",
    )(qi, ki, qa, qb, k, vt, lq1, lk1, lq2, lk2, sub)


def _tail_kernel(h_ref, o_ref, g_ref, p_ref, wo_ref, pn_ref, pg_ref, pp_ref, fn_ref, out_ref):
    gate = g_ref[0]
    z = o_ref[0] * (gate * _sigmoid(gate))
    h = h_ref[0] + _bdot(z, wo_ref[...])
    h = _ple(h, p_ref[0], pn_ref[...], pg_ref[...], pp_ref[...])
    out_ref[0] = _rms(h, fn_ref[...], NORM_EPS)


def _tail(h, o, g, p, wo, pn, pg, pp, fn, tm):
    b, s, d = h.shape
    pd = p.shape[2]
    full = lambda arr: pl.BlockSpec(arr.shape, lambda bi, i: (0,) * arr.ndim)
    row = lambda w: pl.BlockSpec((1, tm, w), lambda bi, i: (bi, i, 0))
    return pl.pallas_call(
        _tail_kernel,
        grid=(b, s // tm),
        in_specs=[row(d), row(d), row(d), row(pd), full(wo), full(pn), full(pg), full(pp), full(fn)],
        out_specs=row(d),
        out_shape=jax.ShapeDtypeStruct((b, s, d), F32),
        compiler_params=_cparams(("parallel", "parallel")),
        name="tail",
    )(h, o, g, p, wo, pn, pg, pp, fn)


def _rope_tables(s):
    dk = DA_QK_DIM
    pos = jnp.arange(s, dtype=F32)
    inv_freq = 1.0 / (ROPE_THETA ** (jnp.arange(0, dk, 2, dtype=F32) / dk))
    ang = pos[:, None] * inv_freq[None, :]
    ang = jnp.concatenate([ang, ang], axis=-1)
    cos = jnp.cos(ang)
    sin = jnp.sin(ang)
    sign = jnp.where(jnp.arange(dk) < dk // 2, -1.0, 1.0).astype(F32)
    sin = sin * sign[None, :]
    return jnp.concatenate([cos, cos], axis=-1), jnp.concatenate([sin, sin], axis=-1)


def kernel(x, p, rw_norm, rw_mu, rw_w_in, rw_w0, rw_w1, rw_w2, rw_a0, rw_a1, rw_a2, rw_k_k, rw_k_a, rw_r_k, rw_ln_w, rw_ln_b, rw_w_out, da_norm, da_w_in, da_lq1, da_lk1, da_lq2, da_lk2, da_subln, da_w_out, pe_norm, pe_w_gate, pe_w_proj, final_norm):
    b, s, d = x.shape
    assert p.shape[0] == 2 and rw_norm.shape[0] == 1 and da_norm.shape[0] == 1
    tm = min(256, s)
    tile = min(ATTN_TILE, s)
    row = lambda vec: vec.reshape(1, -1)
    bf = lambda w: w.astype(BF16)

    r, k, v, gate, lw, a = _rwkv_in(
        x, row(rw_norm[0]), rw_mu[0], bf(rw_w_in[0]), row(rw_w0[0]), bf(rw_w1[0]), bf(rw_w2[0]),
        row(rw_a0[0]), bf(rw_a1[0]), bf(rw_a2[0]), tm)
    y = _rwkv_scan(r, k, v, lw, a, row(rw_k_k[0]), row(rw_k_a[0]), row(rw_r_k[0]), row(rw_ln_w[0]),
                   row(rw_ln_b[0]))
    cos, sin = _rope_tables(s)
    h1, qa, qb, kq, vq, gate2 = _mid(
        x, y, gate, p[0], bf(rw_w_out[0]), row(pe_norm[0]), bf(pe_w_gate[0]), bf(pe_w_proj[0]),
        row(da_norm[0]), bf(da_w_in[0]), cos, sin, tm)
    lambda_init = 0.8 - 0.6 * math.exp(-0.3 * 1)
    o = _attn(qa, qb, kq, vq, row(da_lq1[0]), row(da_lk1[0]), row(da_lq2[0]), row(da_lk2[0]),
              row(da_subln[0]), lambda_init, tile)
    return _tail(h1, o, gate2, p[1], bf(da_w_out[0]), row(pe_norm[1]), bf(pe_w_gate[1]),
                 bf(pe_w_proj[1]), row(final_norm), tm)
```

```python
import functools
import math

import jax
import jax.numpy as jnp
import numpy as np
from jax import lax
from jax.experimental import pallas as pl
from jax.experimental.pallas import tpu as pltpu

F32 = jnp.float32
BF16 = jnp.bfloat16

LANES = 128
MXU_WIDTH = 256
NORM_EPS = 1e-6
GN_EPS = 64e-5
DA_SUBLN_EPS = 1e-5
ROPE_THETA = 10000.0
RW_HEAD = 64
DA_QK_DIM = 64
DA_HEADS = 8
LORA = 64
ROW_TILE = 512
ROW_TILE_MID = 512
SCAN_CHUNK = 64
ATTN_TILE = 1024
ATTN_Q_BLOCK = 256
ATTN_K_BLOCK = 512
ATTN_AHEAD_SCORES = 3
ATTN_AHEAD_SOFTMAX = 1
ATTN_ONES_ROWS = 16
ATTN_HEADS_PER_STEP = 4
VMEM_LIMIT = 56 * 1024 * 1024
EXP_NEG_HALF = math.exp(-0.5)


def _cparams(sem):
    return pltpu.CompilerParams(dimension_semantics=sem, vmem_limit_bytes=VMEM_LIMIT)


def _resident(arr):
    return pl.BlockSpec(arr.shape, lambda *_: (0,) * arr.ndim, pipeline_mode=pl.Buffered(1))


def _rms(x, g, eps):
    return x * lax.rsqrt(jnp.mean(x * x, axis=-1, keepdims=True) + eps) * g


def _sigmoid(x):
    return 1.0 / (1.0 + jnp.exp(-x))


def _bdot(a, b):
    return jnp.dot(a.astype(BF16), b.astype(BF16), preferred_element_type=F32)


def _rwkv_in_kernel(x_ref, xp_ref, nrm_ref, mu_ref, win_ref, w0_ref, w1_ref, w2_ref,
                    a0_ref, a1_ref, a2_ref,
                    r_ref, k_ref, v_ref, g_ref, lw_ref, a_ref):
    i = pl.program_id(1)
    tm = x_ref.shape[1]
    c = r_ref.shape[2]
    nrm = nrm_ref[...]
    hn = _rms(x_ref[0], nrm, NORM_EPS)
    prev_row = _rms(xp_ref[0], nrm, NORM_EPS)[7:8, :]
    prev_row = jnp.where(i == 0, 0.0, prev_row)
    rows = lax.broadcasted_iota(jnp.int32, hn.shape, 0)
    shifted = jnp.where(rows == 0, prev_row, pltpu.roll(hn, 1, 0))
    dx = shifted - hn
    mu = mu_ref[...]

    def lerp(j):
        return (hn + dx * mu[j:j + 1, :]).astype(BF16)

    r_ref[0] = jnp.dot(lerp(0), win_ref[:, 0 * c:1 * c], preferred_element_type=F32)
    k_ref[0] = jnp.dot(lerp(1), win_ref[:, 1 * c:2 * c], preferred_element_type=F32)
    v_ref[0] = jnp.dot(lerp(2), win_ref[:, 2 * c:3 * c], preferred_element_type=F32)
    g_ref[0] = jnp.dot(lerp(3), win_ref[:, 3 * c:4 * c], preferred_element_type=F32)
    tw = jnp.tanh(jnp.dot(lerp(4), w1_ref[...], preferred_element_type=F32))
    zw = w0_ref[...] + _bdot(tw, w2_ref[...])
    lw_ref[0] = -EXP_NEG_HALF * _sigmoid(zw)
    ta = jnp.dot(lerp(5), a1_ref[...], preferred_element_type=F32)
    a_ref[0] = _sigmoid(a0_ref[...] + _bdot(ta, a2_ref[...]))


def _rwkv_in(x, nrm, mu, win, w0, w1, w2, a0, a1, a2, tm):
    b, s, d = x.shape
    c = win.shape[1] // 4
    full = _resident
    out_sds = jax.ShapeDtypeStruct((b, s, c), F32)
    out_spec = pl.BlockSpec((1, tm, c), lambda bi, i: (bi, i, 0))
    return pl.pallas_call(
        _rwkv_in_kernel,
        grid=(b, s // tm),
        in_specs=[
            pl.BlockSpec((1, tm, d), lambda bi, i: (bi, i, 0)),
            pl.BlockSpec((1, 8, d), lambda bi, i: (bi, jnp.maximum(i * (tm // 8) - 1, 0), 0)),
            full(nrm), full(mu), full(win), full(w0), full(w1), full(w2), full(a0), full(a1), full(a2),
        ],
        out_specs=[out_spec] * 6,
        out_shape=[out_sds] * 6,
        compiler_params=_cparams(("parallel", "parallel")),
        name="rwkv_in",
    )(x, x, nrm, mu, win, w0, w1, w2, a0, a1, a2)


def _hi_lo(x):
    hi = x.astype(BF16)
    lo = (x - hi.astype(F32)).astype(BF16)
    return hi, lo


def _cat2(xs, ys, axis):
    return (jnp.concatenate([xs[0], ys[0]], axis=axis), jnp.concatenate([xs[1], ys[1]], axis=axis))


def _dot3(a, b, dims):
    dn = (((dims[0],), (dims[1],)), ((), ()))
    free = 1 - dims[1]
    n = b[0].shape[free]
    if n <= MXU_WIDTH // 2:
        lhs = jnp.concatenate([a[0], a[1]], axis=dims[0])
        top = jnp.concatenate([b[0], b[1]], axis=free)
        bot = jnp.concatenate([b[0], jnp.zeros_like(b[0])], axis=free)
        out = lax.dot_general(lhs, jnp.concatenate([top, bot], axis=dims[1]), dn,
                              preferred_element_type=F32)
        return out[:, :n] + out[:, n:]
    lhs = jnp.concatenate([a[0], a[1], a[0]], axis=dims[0])
    rhs = jnp.concatenate([b[0], b[0], b[1]], axis=dims[1])
    return lax.dot_general(lhs, rhs, dn, preferred_element_type=F32)


def _mm(a, b):
    return _dot3(a, b, (1, 0))


def _mm_nt(a, b):
    return _dot3(a, b, (1, 1))


def _mm_tn(a, b):
    return _dot3(a, b, (0, 0))


def _scan_kernel(r_ref, k_ref, v_ref, lw_ref, a_ref, kk_ref, ka_ref, rk_ref, lnw_ref, lnb_ref,
                 y_ref, st_ref):
    nb, t, c = r_ref.shape
    ppb = c // LANES
    npairs = nb * ppb
    t2 = 2 * t

    @pl.when(pl.program_id(0) == 0)
    def _():
        st_ref[...] = jnp.zeros_like(st_ref)

    wide = lambda ref: jnp.concatenate([ref[b] for b in range(nb)], axis=1)
    tiled = lambda ref: jnp.concatenate([ref[...]] * nb, axis=1)
    r = wide(r_ref)
    kr = wide(k_ref)
    v = wide(v_ref)
    lw = wide(lw_ref)
    a = wide(a_ref)
    lnw = tiled(lnw_ref)
    lnb = tiled(lnb_ref)

    row_t = lax.broadcasted_iota(jnp.int32, (t, t), 0)
    col_t = lax.broadcasted_iota(jnp.int32, (t, t), 1)
    tri = (row_t >= col_t).astype(BF16)
    lw_h = lw.astype(BF16)
    lw_r = lw - lw_h.astype(F32)
    lw_m = lw_r.astype(BF16)
    lw_l = (lw_r - lw_m.astype(F32)).astype(BF16)
    cum = (jnp.dot(tri, lw_h, preferred_element_type=F32) + jnp.dot(tri, lw_m, preferred_element_type=F32)
           + jnp.dot(tri, lw_l, preferred_element_type=F32))
    cum_last = cum[t - 1:t, :]
    e_cum = jnp.exp(cum)
    e_prev = jnp.exp(cum - lw)
    e_inv = jnp.exp(-cum)
    e_tail = jnp.exp(cum_last - cum)
    e_last = jnp.exp(cum_last)
    kku = kr * tiled(kk_ref)
    km = kr * (1.0 + (a - 1.0) * tiled(ka_ref))
    rkr = r * km * tiled(rk_ref)
    r_dec = r * e_cum

    lane = lax.broadcasted_iota(jnp.int32, (1, LANES), 1)
    first = lane < RW_HEAD
    rl = lax.broadcasted_iota(jnp.int32, (LANES, LANES), 0)
    cl = lax.broadcasted_iota(jnp.int32, (LANES, LANES), 1)
    same_head = (rl >> 6) == (cl >> 6)
    head_ones = same_head.astype(BF16)
    head_ones2 = jnp.concatenate([head_ones, head_ones], axis=0)

    def head_sum(x):
        xh, xl = _hi_lo(x)
        return jnp.dot(jnp.concatenate([xh, xl], axis=1), head_ones2, preferred_element_type=F32)

    r2 = lax.broadcasted_iota(jnp.int32, (t2, t2), 0)
    c2 = lax.broadcasted_iota(jnp.int32, (t2, t2), 1)
    strict = (r2 & (t - 1)) > (c2 & (t - 1))
    blk16 = (r2 >> 4) == (c2 >> 4)
    blk32 = (r2 >> 5) == (c2 >> 5)
    eye = (r2 == c2).astype(F32)
    r1 = lax.broadcasted_iota(jnp.int32, (t, t2), 0)
    c1 = lax.broadcasted_iota(jnp.int32, (t, t2), 1)
    incl = r1 >= (c1 & (t - 1))

    zero = jnp.zeros((), BF16)

    def stack(x):
        return tuple(jnp.concatenate([jnp.where(first, h, zero), jnp.where(first, zero, h)], axis=0)
                     for h in x)

    prs = range(npairs)
    sls = [slice(p * LANES, (p + 1) * LANES) for p in prs]
    each = lambda f, *xs: [f(*args) for args in zip(*xs)]

    sums = [head_sum(jnp.concatenate([kku[:, sl] * kku[:, sl], rkr[:, sl]], axis=0)) for sl in sls]
    kk = [kku[:, sl] / jnp.maximum(jnp.sqrt(sm[:t]), 1e-12) for sl, sm in zip(sls, sums)]
    bonus = [sm[t:] * v[:, sl] for sl, sm in zip(sls, sums)]
    bb = [kk_p * a[:, sl] for sl, kk_p in zip(sls, kk)]
    v_p = [_hi_lo(v[:, sl]) for sl in sls]
    r_p = [_hi_lo(r_dec[:, sl]) for sl in sls]
    kk_st = [stack(_hi_lo(kk_p * e_prev[:, sl])) for sl, kk_p in zip(sls, kk)]
    v_st = each(stack, v_p)
    kb_hat = [_cat2(stack(_hi_lo(km[:, sl] * e_inv[:, sl])), stack(_hi_lo(bb_p * e_inv[:, sl])), 0)
              for sl, bb_p in zip(sls, bb)]
    kb_bar = [_hi_lo(jnp.concatenate([km[:, sl] * e_tail[:, sl], -(bb_p * e_tail[:, sl])], axis=0))
              for sl, bb_p in zip(sls, bb)]

    g = [_mm_nt(_cat2(kk_st[p], r_p[p], 0), kb_hat[p]) for p in prs]
    a_k = [_hi_lo(jnp.concatenate([jnp.where(strict, g_p[:t2, :t2], 0.0),
                                   jnp.where(incl, g_p[t2:, :t2], 0.0)], axis=0)) for g_p in g]
    a_kb = [jnp.where(strict, g_p[:t2, t2:], 0.0) for g_p in g]
    a_rb = [_hi_lo(jnp.where(incl, g_p[t2:, t2:], 0.0)) for g_p in g]

    n1 = [jnp.where(blk16, x, 0.0) for x in a_kb]
    n1s = each(_hi_lo, n1)
    ns = [_hi_lo(_mm(x, x)) for x in n1s]
    m = [eye - x for x in n1]
    for _ in range(2):
        prod = [_mm(_cat2(_hi_lo(m_p), n_p, 0), n_p) for m_p, n_p in zip(m, ns)]
        m = [m_p + x[:t2] for m_p, x in zip(m, prod)]
        ns = [_hi_lo(x[t2:]) for x in prod]
    m = [m_p + _mm(_hi_lo(m_p), n_p) for m_p, n_p in zip(m, ns)]
    for hb, sel in ((16, blk32 & jnp.logical_not(blk16)), (32, jnp.logical_not(blk32))):
        lower = lambda x: jnp.concatenate([x[i:i + hb] for i in range(hb, t2, 2 * hb)], axis=0)
        ms = each(_hi_lo, m)
        es = [_hi_lo(jnp.where(sel, x, 0.0)) for x in a_kb]
        me = [_hi_lo(_mm(_hi_lo(lower(m_p)), e_p)) for m_p, e_p in zip(m, es)]
        mem = [_mm(me_p, ms_p) for me_p, ms_p in zip(me, ms)]
        pad = jnp.zeros((hb, t2), F32)
        spread = lambda x: jnp.concatenate(
            [blk for j in range(0, t, hb) for blk in (pad, x[j:j + hb])], axis=0)
        m = [m_p - spread(x) for m_p, x in zip(m, mem)]
    ms = each(_hi_lo, m)

    av = [_mm(a_k[p], v_st[p]) for p in prs]
    w = [_mm(ms[p], _cat2(_hi_lo(av[p][:t2]), kk_st[p], 1)) for p in prs]
    ht = [_hi_lo(st_ref[p]) for p in prs]
    sh = [_mm_nt(_cat2(_hi_lo(w[p][:, LANES:]), r_p[p], 0), ht[p]) for p in prs]
    u_st = [w[p][:, :LANES] + sh[p][:t2] for p in prs]
    u = [x[:t] + x[t:] for x in u_st]
    upd = [_mm_tn(_cat2(v_p[p], _hi_lo(u[p]), 0), kb_bar[p]) for p in prs]
    for p in prs:
        st_ref[p] = st_ref[p] * e_last[:, sls[p]] + jnp.where(same_head, upd[p], 0.0)
    y = [sh[p][t2:] + av[p][t2:] - _mm(a_rb[p], _hi_lo(u_st[p])) for p in prs]

    mean = [head_sum(x) * (1.0 / RW_HEAD) for x in y]
    d = [x - mu for x, mu in zip(y, mean)]
    var = [head_sum(x * x) * (1.0 / RW_HEAD) for x in d]
    for p in prs:
        sl = sls[p]
        yn = d[p] * lax.rsqrt(var[p] + GN_EPS) * lnw[:, sl] + lnb[:, sl]
        y_ref[p // ppb, :, sls[p % ppb]] = yn + bonus[p]


def _rwkv_scan(r, k, v, lw, a, k_k, k_a, r_k, ln_w, ln_b):
    b, s, c = r.shape
    t = SCAN_CHUNK
    seq = pl.BlockSpec((b, t, c), lambda i: (0, i, 0))
    par = pl.BlockSpec((1, c), lambda i: (0, 0))
    return pl.pallas_call(
        _scan_kernel,
        grid=(s // t,),
        in_specs=[seq] * 5 + [par] * 5,
        out_specs=seq,
        out_shape=jax.ShapeDtypeStruct((b, s, c), F32),
        scratch_shapes=[pltpu.VMEM((b * c // LANES, LANES, LANES), F32)],
        compiler_params=_cparams(("arbitrary",)),
        name="rwkv_scan",
    )(r, k, v, lw, a, k_k, k_a, r_k, ln_w, ln_b)


def _ple(h, p, nrm, wg, wp):
    g = _sigmoid(_bdot(_rms(h, nrm, NORM_EPS), wg))
    return h + g * _bdot(p, wp)


def _mid_kernel(x_ref, y_ref, g_ref, p_ref, wo_ref, pn_ref, pg_ref, pp_ref, dn_ref, din_ref,
                cos_ref, sin_ref,
                h_ref, qa_ref, qb_ref, k_ref, v_ref, go_ref):
    nh = qa_ref.shape[1]
    gate = g_ref[0]
    z = y_ref[0] * (gate * _sigmoid(gate))
    h = x_ref[0] + _bdot(z, wo_ref[...])
    h = _ple(h, p_ref[0, 0], pn_ref[...], pg_ref[...], pp_ref[...])
    h_ref[0] = h
    proj = _bdot(_rms(h, dn_ref[...], NORM_EPS), din_ref[...])
    cw = nh * LANES
    cos = cos_ref[...]
    sin = sin_ref[...]
    lane = lax.broadcasted_iota(jnp.int32, (1, LANES), 1)
    low = (lane & (DA_QK_DIM - 1)) < (DA_QK_DIM // 2)
    first = lane < DA_QK_DIM
    scale = DA_QK_DIM ** -0.5 * math.log2(math.e)
    ones = jnp.ones((ATTN_ONES_ROWS, proj.shape[0]), BF16)

    def rope(xh):
        rot = jnp.where(low, pltpu.roll(xh, LANES - DA_QK_DIM // 2, 1), pltpu.roll(xh, DA_QK_DIM // 2, 1))
        return xh * cos + rot * sin

    for hd in range(nh):
        sl = slice(hd * LANES, (hd + 1) * LANES)
        q = rope(proj[:, sl]) * scale
        qa_ref[0, hd] = jnp.where(first, q, 0.0).astype(BF16)
        qb_ref[0, hd] = jnp.where(first, 0.0, q).astype(BF16)
        k_ref[0, hd] = rope(proj[:, cw + hd * LANES:cw + (hd + 1) * LANES]).astype(BF16)
        v_ref[0, hd, :LANES, :] = proj[:, 2 * cw + hd * LANES:2 * cw + (hd + 1) * LANES].T.astype(BF16)
        v_ref[0, hd, LANES:, :] = ones
    go_ref[0] = proj[:, 3 * cw:]


def _mid(x, y, g, p, layer, wo, pn, pg, pp, dn, din, cos, sin, tm):
    b, s, d = x.shape
    nh = DA_HEADS
    p_spec = pl.BlockSpec((1, 1, tm, p.shape[3]), lambda bi, i: (layer, bi, i, 0))
    full = _resident
    row = lambda w: pl.BlockSpec((1, tm, w), lambda bi, i: (bi, i, 0))
    hd_spec = pl.BlockSpec((1, nh, tm, LANES), lambda bi, i: (bi, 0, i, 0))
    hd_sds = jax.ShapeDtypeStruct((b, nh, s, LANES), BF16)
    va_spec = pl.BlockSpec((1, nh, LANES + ATTN_ONES_ROWS, tm), lambda bi, i: (bi, 0, 0, i))
    va_sds = jax.ShapeDtypeStruct((b, nh, LANES + ATTN_ONES_ROWS, s), BF16)
    tab = pl.BlockSpec((tm, LANES), lambda bi, i: (i, 0))
    return pl.pallas_call(
        _mid_kernel,
        grid=(b, s // tm),
        in_specs=[row(d), row(d), row(d), p_spec, full(wo), full(pn), full(pg), full(pp), full(dn),
                  full(din), tab, tab],
        out_specs=[row(d), hd_spec, hd_spec, hd_spec, va_spec, row(d)],
        out_shape=[jax.ShapeDtypeStruct((b, s, d), F32), hd_sds, hd_sds, hd_sds, va_sds,
                   jax.ShapeDtypeStruct((b, s, d), F32)],
        compiler_params=_cparams(("parallel", "parallel")),
        name="mid",
    )(x, y, g, p, wo, pn, pg, pp, dn, din, cos, sin)


def _attn_kernel(qi_ref, ki_ref, qa_ref, qb_ref, k_ref, v_ref, lq1_ref, lk1_ref, lq2_ref, lk2_ref,
                 sub_ref, o_ref, m1_ref, acc1_ref, m2_ref, acc2_ref, *, lambda_init, q_block, k_block):
    step = pl.program_id(2)
    qi = qi_ref[step]
    ki = ki_ref[step]
    nhs = qa_ref.shape[1]
    tq = qa_ref.shape[2]
    tk = k_ref.shape[2]

    @pl.when(ki == 0)
    def _():
        m1_ref[...] = jnp.full_like(m1_ref, -jnp.inf)
        m2_ref[...] = jnp.full_like(m2_ref, -jnp.inf)
        acc1_ref[...] = jnp.zeros_like(acc1_ref)
        acc2_ref[...] = jnp.zeros_like(acc2_ref)

    comps = ((qa_ref, m1_ref, acc1_ref), (qb_ref, m2_ref, acc2_ref))

    def scores(job):
        comp, hh, c0, k0 = job
        return lax.dot_general(k_ref[0, hh, pl.ds(k0, k_block), :], comps[comp][0][0, hh, pl.ds(c0, q_block), :],
                               (((1,), (1,)), ((), ())), preferred_element_type=F32)

    def softmax(job, s, masked):
        comp, hh, c0, k0 = job
        m_ref = comps[comp][1]
        cols = pl.ds(c0, q_block)
        if masked and k0 + k_block > c0 + 1:
            rk = lax.broadcasted_iota(jnp.int32, (k_block, q_block), 0) + k0
            cq = lax.broadcasted_iota(jnp.int32, (k_block, q_block), 1) + c0
            s = jnp.where(rk <= cq, s, -jnp.inf)
        m_old = m_ref[hh, :, cols]
        m_new = jnp.maximum(m_old, jnp.max(s, axis=0, keepdims=True))
        m_ref[hh, :, cols] = m_new
        alpha = jnp.exp2(m_old[:1] - m_new[:1])
        return jnp.exp2(s - m_new[:1]).astype(BF16), alpha

    def accumulate(job, pr, alpha):
        comp, hh, c0, k0 = job
        acc_ref = comps[comp][2]
        cols = pl.ds(c0, q_block)
        pv = jnp.dot(v_ref[0, hh, :, pl.ds(k0, k_block)], pr, preferred_element_type=F32)
        acc_ref[hh, :, cols] = alpha * acc_ref[hh, :, cols] + pv

    def tile_update(masked):
        jobs = [(comp, hh, c0, k0)
                for k0 in range(0, tk, k_block) for c0 in range(0, tq, q_block)
                for hh in range(nhs) for comp in range(2)
                if not (masked and k0 >= c0 + q_block)]
        n = len(jobs)
        s, p = {}, {}
        for i in range(-ATTN_AHEAD_SCORES, n):
            js, jp = i + ATTN_AHEAD_SCORES, i + ATTN_AHEAD_SOFTMAX
            if js < n:
                s[js] = scores(jobs[js])
            if 0 <= jp < n:
                p[jp] = softmax(jobs[jp], s.pop(jp), masked)
            if i >= 0:
                accumulate(jobs[i], *p.pop(i))

    @pl.when(ki < qi)
    def _():
        tile_update(False)

    @pl.when(ki == qi)
    def _():
        tile_update(True)
        lam = (jnp.exp(jnp.sum(lq1_ref[...] * lk1_ref[...], axis=-1, keepdims=True))
               - jnp.exp(jnp.sum(lq2_ref[...] * lk2_ref[...], axis=-1, keepdims=True)) + lambda_init)
        for hh in range(nhs):
            ot = (acc1_ref[hh, :LANES, :] / acc1_ref[hh, LANES:LANES + 1, :]
                  - lam * (acc2_ref[hh, :LANES, :] / acc2_ref[hh, LANES:LANES + 1, :]))
            o_ref[0, :, hh * LANES:(hh + 1) * LANES] = (_rms(ot.T, sub_ref[...], DA_SUBLN_EPS)
                                                        * (1.0 - lambda_init))


def _attn(qa, qb, k, vt, lq1, lk1, lq2, lk2, sub, lambda_init, tile):
    b, nh, dvr, s = vt.shape
    nhs = ATTN_HEADS_PER_STEP
    nq = s // tile
    pairs = [(q, kk) for q in range(nq) for kk in range(q + 1)]
    qi = jnp.asarray(np.array([pq for pq, _ in pairs], np.int32))
    ki = jnp.asarray(np.array([pk for _, pk in pairs], np.int32))
    qspec = pl.BlockSpec((1, nhs, tile, LANES), lambda bi, h, t, qi, ki: (bi, h, qi[t], 0))
    kspec = pl.BlockSpec((1, nhs, tile, LANES), lambda bi, h, t, qi, ki: (bi, h, ki[t], 0))
    vspec = pl.BlockSpec((1, nhs, dvr, tile), lambda bi, h, t, qi, ki: (bi, h, 0, ki[t]))
    small = lambda arr: pl.BlockSpec(arr.shape, lambda bi, h, t, qi, ki: (0,) * arr.ndim)
    grid_spec = pltpu.PrefetchScalarGridSpec(
        num_scalar_prefetch=2,
        grid=(b, nh // nhs, len(pairs)),
        in_specs=[qspec, qspec, kspec, vspec, small(lq1), small(lk1), small(lq2), small(lk2), small(sub)],
        out_specs=pl.BlockSpec((1, tile, nhs * LANES), lambda bi, h, t, qi, ki: (bi, qi[t], h)),
        scratch_shapes=[pltpu.VMEM((nhs, 8, tile), F32), pltpu.VMEM((nhs, dvr, tile), F32)] * 2,
    )
    return pl.pallas_call(
        functools.partial(_attn_kernel, lambda_init=lambda_init, q_block=min(ATTN_Q_BLOCK, tile),
                          k_block=min(ATTN_K_BLOCK, tile)),
        grid_spec=grid_spec,
        out_shape=jax.ShapeDtypeStruct((b, s, nh * LANES), F32),
        compiler_params=_cparams(("parallel", "parallel", "arbitrary")),
        name="diff_attn",
    )(qi, ki, qa, qb, k, vt, lq1, lk1, lq2, lk2, sub)


def _tail_kernel(h_ref, o_ref, g_ref, p_ref, wo_ref, pn_ref, pg_ref, pp_ref, fn_ref, out_ref):
    gate = g_ref[0]
    z = o_ref[0] * (gate * _sigmoid(gate))
    h = h_ref[0] + _bdot(z, wo_ref[...])
    h = _ple(h, p_ref[0, 0], pn_ref[...], pg_ref[...], pp_ref[...])
    out_ref[0] = _rms(h, fn_ref[...], NORM_EPS)


def _tail(h, o, g, p, layer, wo, pn, pg, pp, fn, tm):
    b, s, d = h.shape
    p_spec = pl.BlockSpec((1, 1, tm, p.shape[3]), lambda bi, i: (layer, bi, i, 0))
    full = _resident
    row = lambda w: pl.BlockSpec((1, tm, w), lambda bi, i: (bi, i, 0))
    return pl.pallas_call(
        _tail_kernel,
        grid=(b, s // tm),
        in_specs=[row(d), row(d), row(d), p_spec, full(wo), full(pn), full(pg), full(pp), full(fn)],
        out_specs=row(d),
        out_shape=jax.ShapeDtypeStruct((b, s, d), F32),
        compiler_params=_cparams(("parallel", "parallel")),
        name="tail",
    )(h, o, g, p, wo, pn, pg, pp, fn)


def _rope_tables(s):
    dk = DA_QK_DIM
    pos = jnp.arange(s, dtype=F32)
    inv_freq = 1.0 / (ROPE_THETA ** (jnp.arange(0, dk, 2, dtype=F32) / dk))
    ang = pos[:, None] * inv_freq[None, :]
    cos = jnp.cos(ang)
    sin = jnp.sin(ang)
    reps = LANES // dk
    return (jnp.concatenate([cos, cos] * reps, axis=-1),
            jnp.concatenate([-sin, sin] * reps, axis=-1))


def kernel(x, p, rw_norm, rw_mu, rw_w_in, rw_w0, rw_w1, rw_w2, rw_a0, rw_a1, rw_a2, rw_k_k, rw_k_a, rw_r_k, rw_ln_w, rw_ln_b, rw_w_out, da_norm, da_w_in, da_lq1, da_lk1, da_lq2, da_lk2, da_subln, da_w_out, pe_norm, pe_w_gate, pe_w_proj, final_norm):
    b, s, d = x.shape
    assert p.shape[0] == 2 and rw_norm.shape[0] == 1 and da_norm.shape[0] == 1
    tm = min(ROW_TILE, s)
    tm_mid = min(ROW_TILE_MID, s)
    tile = min(ATTN_TILE, s)
    row = lambda vec: vec.reshape(1, -1)
    bf = lambda w: w.astype(BF16)

    r, k, v, gate, lw, a = _rwkv_in(
        x, row(rw_norm[0]), rw_mu[0], bf(rw_w_in[0]), row(rw_w0[0]), bf(rw_w1[0]), bf(rw_w2[0]),
        row(rw_a0[0]), bf(rw_a1[0]), bf(rw_a2[0]), tm)
    y = _rwkv_scan(r, k, v, lw, a, row(rw_k_k[0]), row(rw_k_a[0]), row(rw_r_k[0]), row(rw_ln_w[0]),
                   row(rw_ln_b[0]))
    cos, sin = _rope_tables(s)
    h1, qa, qb, kq, vq, gate2 = _mid(
        x, y, gate, p, 0, bf(rw_w_out[0]), row(pe_norm[0]), bf(pe_w_gate[0]), bf(pe_w_proj[0]),
        row(da_norm[0]), bf(da_w_in[0]), cos, sin, tm_mid)
    lambda_init = 0.8 - 0.6 * math.exp(-0.3 * 1)
    o = _attn(qa, qb, kq, vq, row(da_lq1[0]), row(da_lk1[0]), row(da_lq2[0]), row(da_lk2[0]),
              row(da_subln[0]), lambda_init, tile)
    return _tail(h1, o, gate2, p, 1, bf(da_w_out[0]), row(pe_norm[1]), bf(pe_w_gate[1]),
                 bf(pe_w_proj[1]), row(final_norm), tm)
```

```python
import functools
import math

import jax
import jax.numpy as jnp
import numpy as np
from jax import lax
from jax.experimental import pallas as pl
from jax.experimental.pallas import tpu as pltpu

F32 = jnp.float32
BF16 = jnp.bfloat16

LANES = 128
MXU_WIDTH = 256
NORM_EPS = 1e-6
GN_EPS = 64e-5
DA_SUBLN_EPS = 1e-5
ROPE_THETA = 10000.0
RW_HEAD = 64
DA_QK_DIM = 64
DA_HEADS = 8
LORA = 64
ROW_TILE = 512
ROW_TILE_MID = 512
SCAN_CHUNK = 64
ATTN_TILE = 1024
ATTN_Q_BLOCK = 256
ATTN_K_BLOCK = 512
ATTN_AHEAD_SCORES = 3
ATTN_AHEAD_SOFTMAX = 1
ATTN_ONES_ROWS = 16
ATTN_HEADS_PER_STEP = 4
VMEM_LIMIT = 56 * 1024 * 1024
EXP_NEG_HALF = math.exp(-0.5)


def _cparams(sem):
    return pltpu.CompilerParams(dimension_semantics=sem, vmem_limit_bytes=VMEM_LIMIT)


def _resident(arr):
    return pl.BlockSpec(arr.shape, lambda *_: (0,) * arr.ndim, pipeline_mode=pl.Buffered(1))


def _rms(x, g, eps):
    return x * lax.rsqrt(jnp.mean(x * x, axis=-1, keepdims=True) + eps) * g


def _sigmoid(x):
    return 1.0 / (1.0 + jnp.exp(-x))


def _bdot(a, b):
    return jnp.dot(a.astype(BF16), b.astype(BF16), preferred_element_type=F32)


def _rwkv_in_kernel(x_ref, xp_ref, nrm_ref, mu_ref, win_ref, w0_ref, w1_ref, w2_ref,
                    a0_ref, a1_ref, a2_ref,
                    r_ref, k_ref, v_ref, g_ref, lw_ref, a_ref):
    i = pl.program_id(1)
    tm = x_ref.shape[1]
    c = r_ref.shape[2]
    nrm = nrm_ref[...]
    hn = _rms(x_ref[0], nrm, NORM_EPS)
    prev_row = _rms(xp_ref[0], nrm, NORM_EPS)[7:8, :]
    prev_row = jnp.where(i == 0, 0.0, prev_row)
    rows = lax.broadcasted_iota(jnp.int32, hn.shape, 0)
    shifted = jnp.where(rows == 0, prev_row, pltpu.roll(hn, 1, 0))
    dx = shifted - hn
    mu = mu_ref[...]

    def lerp(j):
        return (hn + dx * mu[j:j + 1, :]).astype(BF16)

    r_ref[0] = jnp.dot(lerp(0), win_ref[:, 0 * c:1 * c], preferred_element_type=F32)
    k_ref[0] = jnp.dot(lerp(1), win_ref[:, 1 * c:2 * c], preferred_element_type=F32)
    v_ref[0] = jnp.dot(lerp(2), win_ref[:, 2 * c:3 * c], preferred_element_type=F32)
    g_ref[0] = jnp.dot(lerp(3), win_ref[:, 3 * c:4 * c], preferred_element_type=F32)
    tw = jnp.tanh(jnp.dot(lerp(4), w1_ref[...], preferred_element_type=F32))
    zw = w0_ref[...] + _bdot(tw, w2_ref[...])
    lw_ref[0] = -EXP_NEG_HALF * _sigmoid(zw)
    ta = jnp.dot(lerp(5), a1_ref[...], preferred_element_type=F32)
    a_ref[0] = _sigmoid(a0_ref[...] + _bdot(ta, a2_ref[...]))


def _rwkv_in(x, nrm, mu, win, w0, w1, w2, a0, a1, a2, tm):
    b, s, d = x.shape
    c = win.shape[1] // 4
    full = _resident
    out_sds = jax.ShapeDtypeStruct((b, s, c), F32)
    out_spec = pl.BlockSpec((1, tm, c), lambda bi, i: (bi, i, 0))
    return pl.pallas_call(
        _rwkv_in_kernel,
        grid=(b, s // tm),
        in_specs=[
            pl.BlockSpec((1, tm, d), lambda bi, i: (bi, i, 0)),
            pl.BlockSpec((1, 8, d), lambda bi, i: (bi, jnp.maximum(i * (tm // 8) - 1, 0), 0)),
            full(nrm), full(mu), full(win), full(w0), full(w1), full(w2), full(a0), full(a1), full(a2),
        ],
        out_specs=[out_spec] * 6,
        out_shape=[out_sds] * 6,
        compiler_params=_cparams(("parallel", "parallel")),
        name="rwkv_in",
    )(x, x, nrm, mu, win, w0, w1, w2, a0, a1, a2)


def _hi_lo(x):
    hi = x.astype(BF16)
    lo = (x - hi.astype(F32)).astype(BF16)
    return hi, lo


def _cat2(xs, ys, axis):
    return (jnp.concatenate([xs[0], ys[0]], axis=axis), jnp.concatenate([xs[1], ys[1]], axis=axis))


def _dot3(a, b, dims):
    dn = (((dims[0],), (dims[1],)), ((), ()))
    free = 1 - dims[1]
    n = b[0].shape[free]
    if n <= MXU_WIDTH // 2:
        lhs = jnp.concatenate([a[0], a[1]], axis=dims[0])
        top = jnp.concatenate([b[0], b[1]], axis=free)
        bot = jnp.concatenate([b[0], jnp.zeros_like(b[0])], axis=free)
        out = lax.dot_general(lhs, jnp.concatenate([top, bot], axis=dims[1]), dn,
                              preferred_element_type=F32)
        return out[:, :n] + out[:, n:]
    lhs = jnp.concatenate([a[0], a[1], a[0]], axis=dims[0])
    rhs = jnp.concatenate([b[0], b[0], b[1]], axis=dims[1])
    return lax.dot_general(lhs, rhs, dn, preferred_element_type=F32)


def _mm(a, b):
    return _dot3(a, b, (1, 0))


def _mm_nt(a, b):
    return _dot3(a, b, (1, 1))


def _mm_tn(a, b):
    return _dot3(a, b, (0, 0))


def _scan_kernel(r_ref, k_ref, v_ref, lw_ref, a_ref, kk_ref, ka_ref, rk_ref, lnw_ref, lnb_ref,
                 y_ref, st_ref):
    nb, t, c = r_ref.shape
    ppb = c // LANES
    npairs = nb * ppb
    t2 = 2 * t

    @pl.when(pl.program_id(0) == 0)
    def _():
        st_ref[...] = jnp.zeros_like(st_ref)

    wide = lambda ref: jnp.concatenate([ref[b] for b in range(nb)], axis=1)
    tiled = lambda ref: jnp.concatenate([ref[...]] * nb, axis=1)
    r = wide(r_ref)
    kr = wide(k_ref)
    v = wide(v_ref)
    lw = wide(lw_ref)
    a = wide(a_ref)
    lnw = tiled(lnw_ref)
    lnb = tiled(lnb_ref)

    row_t = lax.broadcasted_iota(jnp.int32, (t, t), 0)
    col_t = lax.broadcasted_iota(jnp.int32, (t, t), 1)
    tri = (row_t >= col_t).astype(BF16)
    lw_h = lw.astype(BF16)
    lw_r = lw - lw_h.astype(F32)
    lw_m = lw_r.astype(BF16)
    lw_l = (lw_r - lw_m.astype(F32)).astype(BF16)
    cum = (jnp.dot(tri, lw_h, preferred_element_type=F32) + jnp.dot(tri, lw_m, preferred_element_type=F32)
           + jnp.dot(tri, lw_l, preferred_element_type=F32))
    cum_last = cum[t - 1:t, :]
    e_cum = jnp.exp(cum)
    e_prev = jnp.exp(cum - lw)
    e_inv = jnp.exp(-cum)
    e_tail = jnp.exp(cum_last - cum)
    e_last = jnp.exp(cum_last)
    kku = kr * tiled(kk_ref)
    km = kr * (1.0 + (a - 1.0) * tiled(ka_ref))
    rkr = r * km * tiled(rk_ref)
    r_dec = r * e_cum

    lane = lax.broadcasted_iota(jnp.int32, (1, LANES), 1)
    first = lane < RW_HEAD
    rl = lax.broadcasted_iota(jnp.int32, (LANES, LANES), 0)
    cl = lax.broadcasted_iota(jnp.int32, (LANES, LANES), 1)
    same_head = (rl >> 6) == (cl >> 6)
    head_ones = same_head.astype(BF16)
    head_ones2 = jnp.concatenate([head_ones, head_ones], axis=0)

    def head_sum(x):
        xh, xl = _hi_lo(x)
        return jnp.dot(jnp.concatenate([xh, xl], axis=1), head_ones2, preferred_element_type=F32)

    r2 = lax.broadcasted_iota(jnp.int32, (t2, t2), 0)
    c2 = lax.broadcasted_iota(jnp.int32, (t2, t2), 1)
    tb = t.bit_length() - 1
    strict_bd = ((r2 & (t - 1)) > (c2 & (t - 1))) & ((r2 >> tb) == (c2 >> tb))
    blk16 = (r2 >> 4) == (c2 >> 4)
    blk32 = (r2 >> 5) == (c2 >> 5)
    eye = (r2 == c2).astype(F32)
    r1 = lax.broadcasted_iota(jnp.int32, (t, t2), 0)
    c1 = lax.broadcasted_iota(jnp.int32, (t, t2), 1)
    incl = r1 >= (c1 & (t - 1))

    zero = jnp.zeros((), BF16)

    def stack(x):
        return tuple(jnp.concatenate([jnp.where(first, h, zero), jnp.where(first, zero, h)], axis=0)
                     for h in x)

    prs = range(npairs)
    sls = [slice(p * LANES, (p + 1) * LANES) for p in prs]
    each = lambda f, *xs: [f(*args) for args in zip(*xs)]

    sums = [head_sum(jnp.concatenate([kku[:, sl] * kku[:, sl], rkr[:, sl]], axis=0)) for sl in sls]
    kk = [kku[:, sl] / jnp.maximum(jnp.sqrt(sm[:t]), 1e-12) for sl, sm in zip(sls, sums)]
    bonus = [sm[t:] * v[:, sl] for sl, sm in zip(sls, sums)]
    bb = [kk_p * a[:, sl] for sl, kk_p in zip(sls, kk)]
    v_p = [_hi_lo(v[:, sl]) for sl in sls]
    r_p = [_hi_lo(r_dec[:, sl]) for sl in sls]
    kk_p = [_hi_lo(kk_x * e_prev[:, sl]) for sl, kk_x in zip(sls, kk)]
    kk_st = each(stack, kk_p)
    v_st = each(stack, v_p)
    kb_hat = [_cat2(stack(_hi_lo(km[:, sl] * e_inv[:, sl])), stack(_hi_lo(bb_p * e_inv[:, sl])), 0)
              for sl, bb_p in zip(sls, bb)]
    kb_bar = [_hi_lo(jnp.concatenate([km[:, sl] * e_tail[:, sl], -(bb_p * e_tail[:, sl])], axis=0))
              for sl, bb_p in zip(sls, bb)]

    g = [_mm_nt(_cat2(kk_p[p], r_p[p], 0), kb_hat[p]) for p in prs]
    twice = lambda x: jnp.concatenate([x, x], axis=0)
    a_k = [_hi_lo(jnp.concatenate([jnp.where(strict_bd, twice(g_p[:t, :t2]), 0.0),
                                   jnp.where(incl, g_p[t:, :t2], 0.0)], axis=0)) for g_p in g]
    a_kb = [jnp.where(strict_bd, twice(g_p[:t, t2:]), 0.0) for g_p in g]
    a_rb = [_hi_lo(jnp.where(incl, g_p[t:, t2:], 0.0)) for g_p in g]

    n1 = [jnp.where(blk16, x, 0.0) for x in a_kb]
    n1s = each(_hi_lo, n1)
    ns = [_hi_lo(_mm(x, x)) for x in n1s]
    m = [eye - x for x in n1]
    for _ in range(2):
        prod = [_mm(_cat2(_hi_lo(m_p), n_p, 0), n_p) for m_p, n_p in zip(m, ns)]
        m = [m_p + x[:t2] for m_p, x in zip(m, prod)]
        ns = [_hi_lo(x[t2:]) for x in prod]
    m = [m_p + _mm(_hi_lo(m_p), n_p) for m_p, n_p in zip(m, ns)]
    for hb, sel in ((16, blk32 & jnp.logical_not(blk16)), (32, jnp.logical_not(blk32))):
        lower = lambda x: jnp.concatenate([x[i:i + hb] for i in range(hb, t2, 2 * hb)], axis=0)
        ms = each(_hi_lo, m)
        es = [_hi_lo(jnp.where(sel, x, 0.0)) for x in a_kb]
        me = [_hi_lo(_mm(_hi_lo(lower(m_p)), e_p)) for m_p, e_p in zip(m, es)]
        mem = [_mm(me_p, ms_p) for me_p, ms_p in zip(me, ms)]
        pad = jnp.zeros((hb, t2), F32)
        spread = lambda x: jnp.concatenate(
            [blk for j in range(0, t, hb) for blk in (pad, x[j:j + hb])], axis=0)
        m = [m_p - spread(x) for m_p, x in zip(m, mem)]
    ms = each(_hi_lo, m)

    av = [_mm(a_k[p], v_st[p]) for p in prs]
    w = [_mm(ms[p], _cat2(_hi_lo(av[p][:t2]), kk_st[p], 1)) for p in prs]
    ht = [_hi_lo(st_ref[p]) for p in prs]
    sh = [_mm_nt(_cat2(_hi_lo(w[p][:, LANES:]), r_p[p], 0), ht[p]) for p in prs]
    u_st = [w[p][:, :LANES] + sh[p][:t2] for p in prs]
    u = [x[:t] + x[t:] for x in u_st]
    upd = [_mm_tn(_cat2(v_p[p], _hi_lo(u[p]), 0), kb_bar[p]) for p in prs]
    for p in prs:
        st_ref[p] = st_ref[p] * e_last[:, sls[p]] + jnp.where(same_head, upd[p], 0.0)
    y = [sh[p][t2:] + av[p][t2:] - _mm(a_rb[p], _hi_lo(u_st[p])) for p in prs]

    mean = [head_sum(x) * (1.0 / RW_HEAD) for x in y]
    d = [x - mu for x, mu in zip(y, mean)]
    var = [head_sum(x * x) * (1.0 / RW_HEAD) for x in d]
    for p in prs:
        sl = sls[p]
        yn = d[p] * lax.rsqrt(var[p] + GN_EPS) * lnw[:, sl] + lnb[:, sl]
        y_ref[p // ppb, :, sls[p % ppb]] = yn + bonus[p]


def _rwkv_scan(r, k, v, lw, a, k_k, k_a, r_k, ln_w, ln_b):
    b, s, c = r.shape
    t = SCAN_CHUNK
    seq = pl.BlockSpec((b, t, c), lambda i: (0, i, 0))
    par = pl.BlockSpec((1, c), lambda i: (0, 0))
    return pl.pallas_call(
        _scan_kernel,
        grid=(s // t,),
        in_specs=[seq] * 5 + [par] * 5,
        out_specs=seq,
        out_shape=jax.ShapeDtypeStruct((b, s, c), F32),
        scratch_shapes=[pltpu.VMEM((b * c // LANES, LANES, LANES), F32)],
        compiler_params=_cparams(("arbitrary",)),
        name="rwkv_scan",
    )(r, k, v, lw, a, k_k, k_a, r_k, ln_w, ln_b)


def _ple(h, p, nrm, wg, wp):
    g = _sigmoid(_bdot(_rms(h, nrm, NORM_EPS), wg))
    return h + g * _bdot(p, wp)


def _mid_kernel(x_ref, y_ref, g_ref, p_ref, wo_ref, pn_ref, pg_ref, pp_ref, dn_ref, din_ref,
                cos_ref, sin_ref,
                h_ref, qa_ref, qb_ref, k_ref, v_ref, go_ref):
    nh = qa_ref.shape[1]
    gate = g_ref[0]
    z = y_ref[0] * (gate * _sigmoid(gate))
    h = x_ref[0] + _bdot(z, wo_ref[...])
    h = _ple(h, p_ref[0, 0], pn_ref[...], pg_ref[...], pp_ref[...])
    h_ref[0] = h
    proj = _bdot(_rms(h, dn_ref[...], NORM_EPS), din_ref[...])
    cw = nh * LANES
    cos = cos_ref[...]
    sin = sin_ref[...]
    lane = lax.broadcasted_iota(jnp.int32, (1, LANES), 1)
    low = (lane & (DA_QK_DIM - 1)) < (DA_QK_DIM // 2)
    first = lane < DA_QK_DIM
    scale = DA_QK_DIM ** -0.5 * math.log2(math.e)
    ones = jnp.ones((ATTN_ONES_ROWS, proj.shape[0]), BF16)

    def rope(xh):
        rot = jnp.where(low, pltpu.roll(xh, LANES - DA_QK_DIM // 2, 1), pltpu.roll(xh, DA_QK_DIM // 2, 1))
        return xh * cos + rot * sin

    for hd in range(nh):
        sl = slice(hd * LANES, (hd + 1) * LANES)
        q = rope(proj[:, sl]) * scale
        qa_ref[0, hd] = jnp.where(first, q, 0.0).astype(BF16)
        qb_ref[0, hd] = jnp.where(first, 0.0, q).astype(BF16)
        k_ref[0, hd] = rope(proj[:, cw + hd * LANES:cw + (hd + 1) * LANES]).astype(BF16)
        v_ref[0, hd, :LANES, :] = proj[:, 2 * cw + hd * LANES:2 * cw + (hd + 1) * LANES].T.astype(BF16)
        v_ref[0, hd, LANES:, :] = ones
    go_ref[0] = proj[:, 3 * cw:]


def _mid(x, y, g, p, layer, wo, pn, pg, pp, dn, din, cos, sin, tm):
    b, s, d = x.shape
    nh = DA_HEADS
    p_spec = pl.BlockSpec((1, 1, tm, p.shape[3]), lambda bi, i: (layer, bi, i, 0))
    full = _resident
    row = lambda w: pl.BlockSpec((1, tm, w), lambda bi, i: (bi, i, 0))
    hd_spec = pl.BlockSpec((1, nh, tm, LANES), lambda bi, i: (bi, 0, i, 0))
    hd_sds = jax.ShapeDtypeStruct((b, nh, s, LANES), BF16)
    va_spec = pl.BlockSpec((1, nh, LANES + ATTN_ONES_ROWS, tm), lambda bi, i: (bi, 0, 0, i))
    va_sds = jax.ShapeDtypeStruct((b, nh, LANES + ATTN_ONES_ROWS, s), BF16)
    tab = pl.BlockSpec((tm, LANES), lambda bi, i: (i, 0))
    return pl.pallas_call(
        _mid_kernel,
        grid=(b, s // tm),
        in_specs=[row(d), row(d), row(d), p_spec, full(wo), full(pn), full(pg), full(pp), full(dn),
                  full(din), tab, tab],
        out_specs=[row(d), hd_spec, hd_spec, hd_spec, va_spec, row(d)],
        out_shape=[jax.ShapeDtypeStruct((b, s, d), F32), hd_sds, hd_sds, hd_sds, va_sds,
                   jax.ShapeDtypeStruct((b, s, d), F32)],
        compiler_params=_cparams(("parallel", "parallel")),
        name="mid",
    )(x, y, g, p, wo, pn, pg, pp, dn, din, cos, sin)


def _attn_kernel(qi_ref, ki_ref, qa_ref, qb_ref, k_ref, v_ref, lq1_ref, lk1_ref, lq2_ref, lk2_ref,
                 sub_ref, o_ref, m1_ref, acc1_ref, m2_ref, acc2_ref, *, lambda_init, q_block, k_block):
    step = pl.program_id(2)
    qi = qi_ref[step]
    ki = ki_ref[step]
    nhs = qa_ref.shape[1]
    tq = qa_ref.shape[2]
    tk = k_ref.shape[2]

    @pl.when(ki == 0)
    def _():
        m1_ref[...] = jnp.full_like(m1_ref, -jnp.inf)
        m2_ref[...] = jnp.full_like(m2_ref, -jnp.inf)
        acc1_ref[...] = jnp.zeros_like(acc1_ref)
        acc2_ref[...] = jnp.zeros_like(acc2_ref)

    comps = ((qa_ref, m1_ref, acc1_ref), (qb_ref, m2_ref, acc2_ref))

    def scores(job):
        comp, hh, c0, k0 = job
        return lax.dot_general(k_ref[0, hh, pl.ds(k0, k_block), :], comps[comp][0][0, hh, pl.ds(c0, q_block), :],
                               (((1,), (1,)), ((), ())), preferred_element_type=F32)

    def softmax(job, s, masked):
        comp, hh, c0, k0 = job
        m_ref = comps[comp][1]
        cols = pl.ds(c0, q_block)
        if masked and k0 + k_block > c0 + 1:
            rk = lax.broadcasted_iota(jnp.int32, (k_block, q_block), 0) + k0
            cq = lax.broadcasted_iota(jnp.int32, (k_block, q_block), 1) + c0
            s = jnp.where(rk <= cq, s, -jnp.inf)
        m_old = m_ref[hh, :, cols]
        m_new = jnp.maximum(m_old, jnp.max(s, axis=0, keepdims=True))
        m_ref[hh, :, cols] = m_new
        alpha = jnp.exp2(m_old[:1] - m_new[:1])
        return jnp.exp2(s - m_new[:1]).astype(BF16), alpha

    def accumulate(job, pr, alpha):
        comp, hh, c0, k0 = job
        acc_ref = comps[comp][2]
        cols = pl.ds(c0, q_block)
        pv = jnp.dot(v_ref[0, hh, :, pl.ds(k0, k_block)], pr, preferred_element_type=F32)
        acc_ref[hh, :, cols] = alpha * acc_ref[hh, :, cols] + pv

    def tile_update(masked):
        jobs = [(comp, hh, c0, k0)
                for k0 in range(0, tk, k_block) for c0 in range(0, tq, q_block)
                for hh in range(nhs) for comp in range(2)
                if not (masked and k0 >= c0 + q_block)]
        n = len(jobs)
        s, p = {}, {}
        for i in range(-ATTN_AHEAD_SCORES, n):
            js, jp = i + ATTN_AHEAD_SCORES, i + ATTN_AHEAD_SOFTMAX
            if js < n:
                s[js] = scores(jobs[js])
            if 0 <= jp < n:
                p[jp] = softmax(jobs[jp], s.pop(jp), masked)
            if i >= 0:
                accumulate(jobs[i], *p.pop(i))

    @pl.when(ki < qi)
    def _():
        tile_update(False)

    @pl.when(ki == qi)
    def _():
        tile_update(True)
        lam = (jnp.exp(jnp.sum(lq1_ref[...] * lk1_ref[...], axis=-1, keepdims=True))
               - jnp.exp(jnp.sum(lq2_ref[...] * lk2_ref[...], axis=-1, keepdims=True)) + lambda_init)
        for hh in range(nhs):
            ot = (acc1_ref[hh, :LANES, :] / acc1_ref[hh, LANES:LANES + 1, :]
                  - lam * (acc2_ref[hh, :LANES, :] / acc2_ref[hh, LANES:LANES + 1, :]))
            o_ref[0, :, hh * LANES:(hh + 1) * LANES] = (_rms(ot.T, sub_ref[...], DA_SUBLN_EPS)
                                                        * (1.0 - lambda_init))


def _attn(qa, qb, k, vt, lq1, lk1, lq2, lk2, sub, lambda_init, tile):
    b, nh, dvr, s = vt.shape
    nhs = ATTN_HEADS_PER_STEP
    nq = s // tile
    pairs = [(q, kk) for q in range(nq) for kk in range(q + 1)]
    qi = jnp.asarray(np.array([pq for pq, _ in pairs], np.int32))
    ki = jnp.asarray(np.array([pk for _, pk in pairs], np.int32))
    qspec = pl.BlockSpec((1, nhs, tile, LANES), lambda bi, h, t, qi, ki: (bi, h, qi[t], 0))
    kspec = pl.BlockSpec((1, nhs, tile, LANES), lambda bi, h, t, qi, ki: (bi, h, ki[t], 0))
    vspec = pl.BlockSpec((1, nhs, dvr, tile), lambda bi, h, t, qi, ki: (bi, h, 0, ki[t]))
    small = lambda arr: pl.BlockSpec(arr.shape, lambda bi, h, t, qi, ki: (0,) * arr.ndim)
    grid_spec = pltpu.PrefetchScalarGridSpec(
        num_scalar_prefetch=2,
        grid=(b, nh // nhs, len(pairs)),
        in_specs=[qspec, qspec, kspec, vspec, small(lq1), small(lk1), small(lq2), small(lk2), small(sub)],
        out_specs=pl.BlockSpec((1, tile, nhs * LANES), lambda bi, h, t, qi, ki: (bi, qi[t], h)),
        scratch_shapes=[pltpu.VMEM((nhs, 8, tile), F32), pltpu.VMEM((nhs, dvr, tile), F32)] * 2,
    )
    return pl.pallas_call(
        functools.partial(_attn_kernel, lambda_init=lambda_init, q_block=min(ATTN_Q_BLOCK, tile),
                          k_block=min(ATTN_K_BLOCK, tile)),
        grid_spec=grid_spec,
        out_shape=jax.ShapeDtypeStruct((b, s, nh * LANES), F32),
        compiler_params=_cparams(("parallel", "parallel", "arbitrary")),
        name="diff_attn",
    )(qi, ki, qa, qb, k, vt, lq1, lk1, lq2, lk2, sub)


def _tail_kernel(h_ref, o_ref, g_ref, p_ref, wo_ref, pn_ref, pg_ref, pp_ref, fn_ref, out_ref):
    gate = g_ref[0]
    z = o_ref[0] * (gate * _sigmoid(gate))
    h = h_ref[0] + _bdot(z, wo_ref[...])
    h = _ple(h, p_ref[0, 0], pn_ref[...], pg_ref[...], pp_ref[...])
    out_ref[0] = _rms(h, fn_ref[...], NORM_EPS)


def _tail(h, o, g, p, layer, wo, pn, pg, pp, fn, tm):
    b, s, d = h.shape
    p_spec = pl.BlockSpec((1, 1, tm, p.shape[3]), lambda bi, i: (layer, bi, i, 0))
    full = _resident
    row = lambda w: pl.BlockSpec((1, tm, w), lambda bi, i: (bi, i, 0))
    return pl.pallas_call(
        _tail_kernel,
        grid=(b, s // tm),
        in_specs=[row(d), row(d), row(d), p_spec, full(wo), full(pn), full(pg), full(pp), full(fn)],
        out_specs=row(d),
        out_shape=jax.ShapeDtypeStruct((b, s, d), F32),
        compiler_params=_cparams(("parallel", "parallel")),
        name="tail",
    )(h, o, g, p, wo, pn, pg, pp, fn)


def _rope_tables(s):
    dk = DA_QK_DIM
    pos = jnp.arange(s, dtype=F32)
    inv_freq = 1.0 / (ROPE_THETA ** (jnp.arange(0, dk, 2, dtype=F32) / dk))
    ang = pos[:, None] * inv_freq[None, :]
    cos, sin = lax.optimization_barrier((jnp.cos(ang), jnp.sin(ang)))
    reps = LANES // dk
    return (jnp.concatenate([cos, cos] * reps, axis=-1),
            jnp.concatenate([-sin, sin] * reps, axis=-1))


def kernel(x, p, rw_norm, rw_mu, rw_w_in, rw_w0, rw_w1, rw_w2, rw_a0, rw_a1, rw_a2, rw_k_k, rw_k_a, rw_r_k, rw_ln_w, rw_ln_b, rw_w_out, da_norm, da_w_in, da_lq1, da_lk1, da_lq2, da_lk2, da_subln, da_w_out, pe_norm, pe_w_gate, pe_w_proj, final_norm):
    b, s, d = x.shape
    assert p.shape[0] == 2 and rw_norm.shape[0] == 1 and da_norm.shape[0] == 1
    tm = min(ROW_TILE, s)
    tm_mid = min(ROW_TILE_MID, s)
    tile = min(ATTN_TILE, s)
    row = lambda vec: vec.reshape(1, -1)
    bf = lambda w: w.astype(BF16)

    r, k, v, gate, lw, a = _rwkv_in(
        x, row(rw_norm[0]), rw_mu[0], bf(rw_w_in[0]), row(rw_w0[0]), bf(rw_w1[0]), bf(rw_w2[0]),
        row(rw_a0[0]), bf(rw_a1[0]), bf(rw_a2[0]), tm)
    y = _rwkv_scan(r, k, v, lw, a, row(rw_k_k[0]), row(rw_k_a[0]), row(rw_r_k[0]), row(rw_ln_w[0]),
                   row(rw_ln_b[0]))
    cos, sin = _rope_tables(s)
    h1, qa, qb, kq, vq, gate2 = _mid(
        x, y, gate, p, 0, bf(rw_w_out[0]), row(pe_norm[0]), bf(pe_w_gate[0]), bf(pe_w_proj[0]),
        row(da_norm[0]), bf(da_w_in[0]), cos, sin, tm_mid)
    lambda_init = 0.8 - 0.6 * math.exp(-0.3 * 1)
    o = _attn(qa, qb, kq, vq, row(da_lq1[0]), row(da_lk1[0]), row(da_lq2[0]), row(da_lk2[0]),
              row(da_subln[0]), lambda_init, tile)
    return _tail(h1, o, gate2, p, 1, bf(da_w_out[0]), row(pe_norm[1]), bf(pe_w_gate[1]),
                 bf(pe_w_proj[1]), row(final_norm), tm)
```

```python
import functools
import math

import jax
import jax.numpy as jnp
import numpy as np
from jax import lax
from jax.experimental import pallas as pl
from jax.experimental.pallas import tpu as pltpu

F32 = jnp.float32
BF16 = jnp.bfloat16

LANES = 128
MXU_WIDTH = 256
NORM_EPS = 1e-6
GN_EPS = 64e-5
DA_SUBLN_EPS = 1e-5
ROPE_THETA = 10000.0
RW_HEAD = 64
DA_QK_DIM = 64
DA_HEADS = 8
LORA = 64
ROW_TILE = 512
ROW_TILE_MID = 512
SCAN_CHUNK = 64
ATTN_TILE = 1024
ATTN_Q_BLOCK = 256
ATTN_K_BLOCK = 512
ATTN_AHEAD_SCORES = 3
ATTN_AHEAD_SOFTMAX = 1
ATTN_ONES_ROWS = 16
ATTN_HEADS_PER_STEP = 4
VMEM_LIMIT = 56 * 1024 * 1024
EXP_NEG_HALF = math.exp(-0.5)


def _cparams(sem):
    return pltpu.CompilerParams(dimension_semantics=sem, vmem_limit_bytes=VMEM_LIMIT)


def _resident(arr):
    return pl.BlockSpec(arr.shape, lambda *_: (0,) * arr.ndim, pipeline_mode=pl.Buffered(1))


def _rms(x, g, eps):
    return x * lax.rsqrt(jnp.mean(x * x, axis=-1, keepdims=True) + eps) * g


def _sigmoid(x):
    return 1.0 / (1.0 + jnp.exp(-x))


def _bdot(a, b):
    return jnp.dot(a.astype(BF16), b.astype(BF16), preferred_element_type=F32)


def _rwkv_in_kernel(x_ref, xp_ref, nrm_ref, mu_ref, win_ref, w0_ref, w1_ref, w2_ref,
                    a0_ref, a1_ref, a2_ref,
                    r_ref, k_ref, v_ref, g_ref, lw_ref, a_ref):
    i = pl.program_id(1)
    tm = x_ref.shape[1]
    c = r_ref.shape[2]
    nrm = nrm_ref[...]
    hn = _rms(x_ref[0], nrm, NORM_EPS)
    prev_row = _rms(xp_ref[0], nrm, NORM_EPS)[7:8, :]
    prev_row = jnp.where(i == 0, 0.0, prev_row)
    rows = lax.broadcasted_iota(jnp.int32, hn.shape, 0)
    shifted = jnp.where(rows == 0, prev_row, pltpu.roll(hn, 1, 0))
    dx = shifted - hn
    mu = mu_ref[...]

    def lerp(j):
        return (hn + dx * mu[j:j + 1, :]).astype(BF16)

    r_ref[0] = jnp.dot(lerp(0), win_ref[:, 0 * c:1 * c], preferred_element_type=F32)
    k_ref[0] = jnp.dot(lerp(1), win_ref[:, 1 * c:2 * c], preferred_element_type=F32)
    v_ref[0] = jnp.dot(lerp(2), win_ref[:, 2 * c:3 * c], preferred_element_type=F32)
    g_ref[0] = jnp.dot(lerp(3), win_ref[:, 3 * c:4 * c], preferred_element_type=F32)
    tw = jnp.tanh(jnp.dot(lerp(4), w1_ref[...], preferred_element_type=F32))
    zw = w0_ref[...] + _bdot(tw, w2_ref[...])
    lw_ref[0] = -EXP_NEG_HALF * _sigmoid(zw)
    ta = jnp.dot(lerp(5), a1_ref[...], preferred_element_type=F32)
    a_ref[0] = _sigmoid(a0_ref[...] + _bdot(ta, a2_ref[...]))


def _rwkv_in(x, nrm, mu, win, w0, w1, w2, a0, a1, a2, tm):
    b, s, d = x.shape
    c = win.shape[1] // 4
    full = _resident
    out_sds = jax.ShapeDtypeStruct((b, s, c), F32)
    out_spec = pl.BlockSpec((1, tm, c), lambda bi, i: (bi, i, 0))
    return pl.pallas_call(
        _rwkv_in_kernel,
        grid=(b, s // tm),
        in_specs=[
            pl.BlockSpec((1, tm, d), lambda bi, i: (bi, i, 0)),
            pl.BlockSpec((1, 8, d), lambda bi, i: (bi, jnp.maximum(i * (tm // 8) - 1, 0), 0)),
            full(nrm), full(mu), full(win), full(w0), full(w1), full(w2), full(a0), full(a1), full(a2),
        ],
        out_specs=[out_spec] * 6,
        out_shape=[out_sds] * 6,
        compiler_params=_cparams(("parallel", "parallel")),
        name="rwkv_in",
    )(x, x, nrm, mu, win, w0, w1, w2, a0, a1, a2)


def _hi_lo(x):
    hi = x.astype(BF16)
    lo = (x - hi.astype(F32)).astype(BF16)
    return hi, lo


def _cat2(xs, ys, axis):
    return (jnp.concatenate([xs[0], ys[0]], axis=axis), jnp.concatenate([xs[1], ys[1]], axis=axis))


def _dot3(a, b, dims):
    dn = (((dims[0],), (dims[1],)), ((), ()))
    free = 1 - dims[1]
    n = b[0].shape[free]
    if n <= MXU_WIDTH // 2:
        lhs = jnp.concatenate([a[0], a[1]], axis=dims[0])
        top = jnp.concatenate([b[0], b[1]], axis=free)
        bot = jnp.concatenate([b[0], jnp.zeros_like(b[0])], axis=free)
        out = lax.dot_general(lhs, jnp.concatenate([top, bot], axis=dims[1]), dn,
                              preferred_element_type=F32)
        return out[:, :n] + out[:, n:]
    lhs = jnp.concatenate([a[0], a[1], a[0]], axis=dims[0])
    rhs = jnp.concatenate([b[0], b[0], b[1]], axis=dims[1])
    return lax.dot_general(lhs, rhs, dn, preferred_element_type=F32)


def _mm(a, b):
    return _dot3(a, b, (1, 0))


def _mm_nt(a, b):
    return _dot3(a, b, (1, 1))


def _mm_tn(a, b):
    return _dot3(a, b, (0, 0))


def _scan_kernel(r_ref, k_ref, v_ref, lw_ref, a_ref, kk_ref, ka_ref, rk_ref, lnw_ref, lnb_ref,
                 y_ref, st_ref):
    nb, t, c = r_ref.shape
    ppb = c // LANES
    npairs = nb * ppb
    t2 = 2 * t

    @pl.when(pl.program_id(0) == 0)
    def _():
        st_ref[...] = jnp.zeros_like(st_ref)

    wide = lambda ref: jnp.concatenate([ref[b] for b in range(nb)], axis=1)
    tiled = lambda ref: jnp.concatenate([ref[...]] * nb, axis=1)
    r = wide(r_ref)
    kr = wide(k_ref)
    v = wide(v_ref)
    lw = wide(lw_ref)
    a = wide(a_ref)
    lnw = tiled(lnw_ref)
    lnb = tiled(lnb_ref)

    row_t = lax.broadcasted_iota(jnp.int32, (t, t), 0)
    col_t = lax.broadcasted_iota(jnp.int32, (t, t), 1)
    tri = (row_t >= col_t).astype(BF16)
    lw_h = lw.astype(BF16)
    lw_r = lw - lw_h.astype(F32)
    lw_m = lw_r.astype(BF16)
    lw_l = (lw_r - lw_m.astype(F32)).astype(BF16)
    cum = (jnp.dot(tri, lw_h, preferred_element_type=F32) + jnp.dot(tri, lw_m, preferred_element_type=F32)
           + jnp.dot(tri, lw_l, preferred_element_type=F32))
    cum_last = cum[t - 1:t, :]
    e_cum = jnp.exp(cum)
    e_prev = jnp.exp(cum - lw)
    e_inv = jnp.exp(-cum)
    e_tail = jnp.exp(cum_last - cum)
    e_last = jnp.exp(cum_last)
    kku = kr * tiled(kk_ref)
    km = kr * (1.0 + (a - 1.0) * tiled(ka_ref))
    rkr = r * km * tiled(rk_ref)
    r_dec = r * e_cum

    lane = lax.broadcasted_iota(jnp.int32, (1, LANES), 1)
    first = lane < RW_HEAD
    rl = lax.broadcasted_iota(jnp.int32, (LANES, LANES), 0)
    cl = lax.broadcasted_iota(jnp.int32, (LANES, LANES), 1)
    same_head = (rl >> 6) == (cl >> 6)
    head_ones = same_head.astype(BF16)
    head_ones2 = jnp.concatenate([head_ones, head_ones], axis=0)

    def head_sum(x):
        xh, xl = _hi_lo(x)
        return jnp.dot(jnp.concatenate([xh, xl], axis=1), head_ones2, preferred_element_type=F32)

    r2 = lax.broadcasted_iota(jnp.int32, (t2, t2), 0)
    c2 = lax.broadcasted_iota(jnp.int32, (t2, t2), 1)
    tb = t.bit_length() - 1
    strict_bd = ((r2 & (t - 1)) > (c2 & (t - 1))) & ((r2 >> tb) == (c2 >> tb))
    blk16 = (r2 >> 4) == (c2 >> 4)
    blk32 = (r2 >> 5) == (c2 >> 5)
    eye = (r2 == c2).astype(F32)
    r1 = lax.broadcasted_iota(jnp.int32, (t, t2), 0)
    c1 = lax.broadcasted_iota(jnp.int32, (t, t2), 1)
    incl = r1 >= (c1 & (t - 1))

    zero = jnp.zeros((), BF16)

    def stack(x):
        return tuple(jnp.concatenate([jnp.where(first, h, zero), jnp.where(first, zero, h)], axis=0)
                     for h in x)

    prs = range(npairs)
    sls = [slice(p * LANES, (p + 1) * LANES) for p in prs]
    each = lambda f, *xs: [f(*args) for args in zip(*xs)]

    sums = [head_sum(jnp.concatenate([kku[:, sl] * kku[:, sl], rkr[:, sl]], axis=0)) for sl in sls]
    kk = [kku[:, sl] / jnp.maximum(jnp.sqrt(sm[:t]), 1e-12) for sl, sm in zip(sls, sums)]
    bonus = [sm[t:] * v[:, sl] for sl, sm in zip(sls, sums)]
    bb = [kk_p * a[:, sl] for sl, kk_p in zip(sls, kk)]
    v_p = [_hi_lo(v[:, sl]) for sl in sls]
    r_p = [_hi_lo(r_dec[:, sl]) for sl in sls]
    kk_p = [_hi_lo(kk_x * e_prev[:, sl]) for sl, kk_x in zip(sls, kk)]
    kk_st = each(stack, kk_p)
    v_st = each(stack, v_p)
    kb_hat = [_cat2(stack(_hi_lo(km[:, sl] * e_inv[:, sl])), stack(_hi_lo(bb_p * e_inv[:, sl])), 0)
              for sl, bb_p in zip(sls, bb)]
    kb_bar = [_hi_lo(jnp.concatenate([km[:, sl] * e_tail[:, sl], -(bb_p * e_tail[:, sl])], axis=0))
              for sl, bb_p in zip(sls, bb)]

    g = [_mm_nt(_cat2(kk_p[p], r_p[p], 0), kb_hat[p]) for p in prs]
    twice = lambda x: jnp.concatenate([x, x], axis=0)
    a_k = [_hi_lo(jnp.concatenate([jnp.where(strict_bd, twice(g_p[:t, :t2]), 0.0),
                                   jnp.where(incl, g_p[t:, :t2], 0.0)], axis=0)) for g_p in g]
    a_kb = [jnp.where(strict_bd, twice(g_p[:t, t2:]), 0.0) for g_p in g]
    a_rb = [_hi_lo(jnp.where(incl, g_p[t:, t2:], 0.0)) for g_p in g]

    n1 = [jnp.where(blk16, x, 0.0) for x in a_kb]
    n1s = each(_hi_lo, n1)
    ns = [_hi_lo(_mm(x, x)) for x in n1s]
    m = [eye - x for x in n1]
    for _ in range(2):
        prod = [_mm(_cat2(_hi_lo(m_p), n_p, 0), n_p) for m_p, n_p in zip(m, ns)]
        m = [m_p + x[:t2] for m_p, x in zip(m, prod)]
        ns = [_hi_lo(x[t2:]) for x in prod]
    m = [m_p + _mm(_hi_lo(m_p), n_p) for m_p, n_p in zip(m, ns)]
    for hb, sel in ((16, blk32 & jnp.logical_not(blk16)), (32, jnp.logical_not(blk32))):
        lower = lambda x: jnp.concatenate([x[i:i + hb] for i in range(hb, t2, 2 * hb)], axis=0)
        ms = each(_hi_lo, m)
        es = [_hi_lo(jnp.where(sel, x, 0.0)) for x in a_kb]
        me = [_hi_lo(_mm(_hi_lo(lower(m_p)), e_p)) for m_p, e_p in zip(m, es)]
        mem = [_mm(me_p, ms_p) for me_p, ms_p in zip(me, ms)]
        pad = jnp.zeros((hb, t2), F32)
        spread = lambda x: jnp.concatenate(
            [blk for j in range(0, t, hb) for blk in (pad, x[j:j + hb])], axis=0)
        m = [m_p - spread(x) for m_p, x in zip(m, mem)]
    ms = each(_hi_lo, m)

    av = [_mm(a_k[p], v_st[p]) for p in prs]
    w = [_mm(ms[p], _cat2(_hi_lo(av[p][:t2]), kk_st[p], 1)) for p in prs]
    ht = [_hi_lo(st_ref[p]) for p in prs]
    sh = [_mm_nt(_cat2(_hi_lo(w[p][:, LANES:]), r_p[p], 0), ht[p]) for p in prs]
    u_st = [w[p][:, :LANES] + sh[p][:t2] for p in prs]
    u = [x[:t] + x[t:] for x in u_st]
    upd = [_mm_tn(_cat2(v_p[p], _hi_lo(u[p]), 0), kb_bar[p]) for p in prs]
    for p in prs:
        st_ref[p] = st_ref[p] * e_last[:, sls[p]] + jnp.where(same_head, upd[p], 0.0)
    y = [sh[p][t2:] + av[p][t2:] - _mm(a_rb[p], _hi_lo(u_st[p])) for p in prs]

    mean = [head_sum(x) * (1.0 / RW_HEAD) for x in y]
    d = [x - mu for x, mu in zip(y, mean)]
    var = [head_sum(x * x) * (1.0 / RW_HEAD) for x in d]
    for p in prs:
        sl = sls[p]
        yn = d[p] * lax.rsqrt(var[p] + GN_EPS) * lnw[:, sl] + lnb[:, sl]
        y_ref[p // ppb, :, sls[p % ppb]] = yn + bonus[p]


def _rwkv_scan(r, k, v, lw, a, k_k, k_a, r_k, ln_w, ln_b):
    b, s, c = r.shape
    t = SCAN_CHUNK
    seq = pl.BlockSpec((b, t, c), lambda i: (0, i, 0))
    par = pl.BlockSpec((1, c), lambda i: (0, 0))
    return pl.pallas_call(
        _scan_kernel,
        grid=(s // t,),
        in_specs=[seq] * 5 + [par] * 5,
        out_specs=seq,
        out_shape=jax.ShapeDtypeStruct((b, s, c), F32),
        scratch_shapes=[pltpu.VMEM((b * c // LANES, LANES, LANES), F32)],
        compiler_params=_cparams(("arbitrary",)),
        name="rwkv_scan",
    )(r, k, v, lw, a, k_k, k_a, r_k, ln_w, ln_b)


def _ple(h, p, nrm, wg, wp):
    g = _sigmoid(_bdot(_rms(h, nrm, NORM_EPS), wg))
    return h + g * _bdot(p, wp)


def _mid_kernel(x_ref, y_ref, g_ref, p_ref, wo_ref, pn_ref, pg_ref, pp_ref, dn_ref, din_ref,
                cos_ref, sin_ref,
                h_ref, qa_ref, qb_ref, k_ref, v_ref, go_ref):
    nh = qa_ref.shape[1]
    gate = g_ref[0]
    z = y_ref[0] * (gate * _sigmoid(gate))
    h = x_ref[0] + _bdot(z, wo_ref[...])
    h = _ple(h, p_ref[0, 0], pn_ref[...], pg_ref[...], pp_ref[...])
    h_ref[0] = h
    proj = _bdot(_rms(h, dn_ref[...], NORM_EPS), din_ref[...])
    cw = nh * LANES
    cos = cos_ref[...]
    sin = sin_ref[...]
    lane = lax.broadcasted_iota(jnp.int32, (1, LANES), 1)
    low = (lane & (DA_QK_DIM - 1)) < (DA_QK_DIM // 2)
    first = lane < DA_QK_DIM
    scale = DA_QK_DIM ** -0.5 * math.log2(math.e)
    ones = jnp.ones((ATTN_ONES_ROWS, proj.shape[0]), BF16)

    def rope(xh):
        rot = jnp.where(low, pltpu.roll(xh, LANES - DA_QK_DIM // 2, 1), pltpu.roll(xh, DA_QK_DIM // 2, 1))
        return xh * cos + rot * sin

    for hd in range(nh):
        sl = slice(hd * LANES, (hd + 1) * LANES)
        q = rope(proj[:, sl]) * scale
        qa_ref[0, hd] = jnp.where(first, q, 0.0).astype(BF16)
        qb_ref[0, hd] = jnp.where(first, 0.0, q).astype(BF16)
        k_ref[0, hd] = rope(proj[:, cw + hd * LANES:cw + (hd + 1) * LANES]).astype(BF16)
        v_ref[0, hd, :LANES, :] = proj[:, 2 * cw + hd * LANES:2 * cw + (hd + 1) * LANES].T.astype(BF16)
        v_ref[0, hd, LANES:, :] = ones
    go_ref[0] = proj[:, 3 * cw:]


def _mid(x, y, g, p, layer, wo, pn, pg, pp, dn, din, cos, sin, tm):
    b, s, d = x.shape
    nh = DA_HEADS
    p_spec = pl.BlockSpec((1, 1, tm, p.shape[3]), lambda bi, i: (layer, bi, i, 0))
    full = _resident
    row = lambda w: pl.BlockSpec((1, tm, w), lambda bi, i: (bi, i, 0))
    hd_spec = pl.BlockSpec((1, nh, tm, LANES), lambda bi, i: (bi, 0, i, 0))
    hd_sds = jax.ShapeDtypeStruct((b, nh, s, LANES), BF16)
    va_spec = pl.BlockSpec((1, nh, LANES + ATTN_ONES_ROWS, tm), lambda bi, i: (bi, 0, 0, i))
    va_sds = jax.ShapeDtypeStruct((b, nh, LANES + ATTN_ONES_ROWS, s), BF16)
    tab = pl.BlockSpec((tm, LANES), lambda bi, i: (i, 0))
    return pl.pallas_call(
        _mid_kernel,
        grid=(b, s // tm),
        in_specs=[row(d), row(d), row(d), p_spec, full(wo), full(pn), full(pg), full(pp), full(dn),
                  full(din), tab, tab],
        out_specs=[row(d), hd_spec, hd_spec, hd_spec, va_spec, row(d)],
        out_shape=[jax.ShapeDtypeStruct((b, s, d), F32), hd_sds, hd_sds, hd_sds, va_sds,
                   jax.ShapeDtypeStruct((b, s, d), F32)],
        compiler_params=_cparams(("parallel", "parallel")),
        name="mid",
    )(x, y, g, p, wo, pn, pg, pp, dn, din, cos, sin)


def _attn_kernel(qi_ref, ki_ref, qa_ref, qb_ref, k_ref, v_ref, lq1_ref, lk1_ref, lq2_ref, lk2_ref,
                 sub_ref, o_ref, m1_ref, acc1_ref, m2_ref, acc2_ref, s_ref, *, lambda_init, q_block, k_block):
    step = pl.program_id(2)
    qi = qi_ref[step]
    ki = ki_ref[step]
    nhs = qa_ref.shape[1]
    tq = qa_ref.shape[2]
    tk = k_ref.shape[2]

    @pl.when(ki == 0)
    def _():
        m1_ref[...] = jnp.full_like(m1_ref, -jnp.inf)
        m2_ref[...] = jnp.full_like(m2_ref, -jnp.inf)
        acc1_ref[...] = jnp.zeros_like(acc1_ref)
        acc2_ref[...] = jnp.zeros_like(acc2_ref)

    comps = ((qa_ref, m1_ref, acc1_ref), (qb_ref, m2_ref, acc2_ref))

    def scores(job):
        comp, hh, c0, k0 = job
        return lax.dot_general(k_ref[0, hh, pl.ds(k0, k_block), :], comps[comp][0][0, hh, pl.ds(c0, q_block), :],
                               (((1,), (1,)), ((), ())), preferred_element_type=F32)

    def softmax(job, s, masked):
        comp, hh, c0, k0 = job
        m_ref = comps[comp][1]
        cols = pl.ds(c0, q_block)
        if masked and k0 + k_block > c0 + 1:
            rk = lax.broadcasted_iota(jnp.int32, (k_block, q_block), 0) + k0
            cq = lax.broadcasted_iota(jnp.int32, (k_block, q_block), 1) + c0
            s = jnp.where(rk <= cq, s, -jnp.inf)
        m_old = m_ref[hh, :, cols]
        m_new = jnp.maximum(m_old, jnp.max(s, axis=0, keepdims=True))
        m_ref[hh, :, cols] = m_new
        alpha = jnp.exp2(m_old[:1] - m_new[:1])
        return jnp.exp2(s - m_new[:1]).astype(BF16), alpha

    def accumulate(job, pr, alpha):
        comp, hh, c0, k0 = job
        acc_ref = comps[comp][2]
        cols = pl.ds(c0, q_block)
        pv = jnp.dot(v_ref[0, hh, :, pl.ds(k0, k_block)], pr, preferred_element_type=F32)
        acc_ref[hh, :, cols] = alpha * acc_ref[hh, :, cols] + pv

    def tile_update(masked):
        jobs = [(comp, hh, c0, k0)
                for k0 in range(0, tk, k_block) for c0 in range(0, tq, q_block)
                for hh in range(nhs) for comp in range(2)
                if not (masked and k0 >= c0 + q_block)]
        n = len(jobs)
        p = {}
        nslots = s_ref.shape[0]
        for i in range(-ATTN_AHEAD_SCORES, n):
            js, jp = i + ATTN_AHEAD_SCORES, i + ATTN_AHEAD_SOFTMAX
            if js < n:
                s_ref[js % nslots] = scores(jobs[js])
            if 0 <= jp < n:
                p[jp] = softmax(jobs[jp], s_ref[jp % nslots], masked)
            if i >= 0:
                accumulate(jobs[i], *p.pop(i))

    @pl.when(ki < qi)
    def _():
        tile_update(False)

    @pl.when(ki == qi)
    def _():
        tile_update(True)
        lam = (jnp.exp(jnp.sum(lq1_ref[...] * lk1_ref[...], axis=-1, keepdims=True))
               - jnp.exp(jnp.sum(lq2_ref[...] * lk2_ref[...], axis=-1, keepdims=True)) + lambda_init)
        for hh in range(nhs):
            ot = (acc1_ref[hh, :LANES, :] / acc1_ref[hh, LANES:LANES + 1, :]
                  - lam * (acc2_ref[hh, :LANES, :] / acc2_ref[hh, LANES:LANES + 1, :]))
            o_ref[0, :, hh * LANES:(hh + 1) * LANES] = (_rms(ot.T, sub_ref[...], DA_SUBLN_EPS)
                                                        * (1.0 - lambda_init))


def _attn(qa, qb, k, vt, lq1, lk1, lq2, lk2, sub, lambda_init, tile):
    b, nh, dvr, s = vt.shape
    nhs = ATTN_HEADS_PER_STEP
    nq = s // tile
    pairs = [(q, kk) for q in range(nq) for kk in range(q + 1)]
    qi = jnp.asarray(np.array([pq for pq, _ in pairs], np.int32))
    ki = jnp.asarray(np.array([pk for _, pk in pairs], np.int32))
    qspec = pl.BlockSpec((1, nhs, tile, LANES), lambda bi, h, t, qi, ki: (bi, h, qi[t], 0))
    kspec = pl.BlockSpec((1, nhs, tile, LANES), lambda bi, h, t, qi, ki: (bi, h, ki[t], 0))
    vspec = pl.BlockSpec((1, nhs, dvr, tile), lambda bi, h, t, qi, ki: (bi, h, 0, ki[t]))
    small = lambda arr: pl.BlockSpec(arr.shape, lambda bi, h, t, qi, ki: (0,) * arr.ndim)
    grid_spec = pltpu.PrefetchScalarGridSpec(
        num_scalar_prefetch=2,
        grid=(b, nh // nhs, len(pairs)),
        in_specs=[qspec, qspec, kspec, vspec, small(lq1), small(lk1), small(lq2), small(lk2), small(sub)],
        out_specs=pl.BlockSpec((1, tile, nhs * LANES), lambda bi, h, t, qi, ki: (bi, qi[t], h)),
        scratch_shapes=[pltpu.VMEM((nhs, 8, tile), F32), pltpu.VMEM((nhs, dvr, tile), F32)] * 2
        + [pltpu.VMEM((ATTN_AHEAD_SCORES - ATTN_AHEAD_SOFTMAX + 1, min(ATTN_K_BLOCK, tile),
                       min(ATTN_Q_BLOCK, tile)), F32)],
    )
    return pl.pallas_call(
        functools.partial(_attn_kernel, lambda_init=lambda_init, q_block=min(ATTN_Q_BLOCK, tile),
                          k_block=min(ATTN_K_BLOCK, tile)),
        grid_spec=grid_spec,
        out_shape=jax.ShapeDtypeStruct((b, s, nh * LANES), F32),
        compiler_params=_cparams(("parallel", "parallel", "arbitrary")),
        name="diff_attn",
    )(qi, ki, qa, qb, k, vt, lq1, lk1, lq2, lk2, sub)


def _tail_kernel(h_ref, o_ref, g_ref, p_ref, wo_ref, pn_ref, pg_ref, pp_ref, fn_ref, out_ref):
    gate = g_ref[0]
    z = o_ref[0] * (gate * _sigmoid(gate))
    h = h_ref[0] + _bdot(z, wo_ref[...])
    h = _ple(h, p_ref[0, 0], pn_ref[...], pg_ref[...], pp_ref[...])
    out_ref[0] = _rms(h, fn_ref[...], NORM_EPS)


def _tail(h, o, g, p, layer, wo, pn, pg, pp, fn, tm):
    b, s, d = h.shape
    p_spec = pl.BlockSpec((1, 1, tm, p.shape[3]), lambda bi, i: (layer, bi, i, 0))
    full = _resident
    row = lambda w: pl.BlockSpec((1, tm, w), lambda bi, i: (bi, i, 0))
    return pl.pallas_call(
        _tail_kernel,
        grid=(b, s // tm),
        in_specs=[row(d), row(d), row(d), p_spec, full(wo), full(pn), full(pg), full(pp), full(fn)],
        out_specs=row(d),
        out_shape=jax.ShapeDtypeStruct((b, s, d), F32),
        compiler_params=_cparams(("parallel", "parallel")),
        name="tail",
    )(h, o, g, p, wo, pn, pg, pp, fn)


def _rope_tables(s):
    dk = DA_QK_DIM
    pos = jnp.arange(s, dtype=F32)
    inv_freq = 1.0 / (ROPE_THETA ** (jnp.arange(0, dk, 2, dtype=F32) / dk))
    ang = pos[:, None] * inv_freq[None, :]
    cos, sin = lax.optimization_barrier((jnp.cos(ang), jnp.sin(ang)))
    reps = LANES // dk
    return (jnp.concatenate([cos, cos] * reps, axis=-1),
            jnp.concatenate([-sin, sin] * reps, axis=-1))


def kernel(x, p, rw_norm, rw_mu, rw_w_in, rw_w0, rw_w1, rw_w2, rw_a0, rw_a1, rw_a2, rw_k_k, rw_k_a, rw_r_k, rw_ln_w, rw_ln_b, rw_w_out, da_norm, da_w_in, da_lq1, da_lk1, da_lq2, da_lk2, da_subln, da_w_out, pe_norm, pe_w_gate, pe_w_proj, final_norm):
    b, s, d = x.shape
    assert p.shape[0] == 2 and rw_norm.shape[0] == 1 and da_norm.shape[0] == 1
    tm = min(ROW_TILE, s)
    tm_mid = min(ROW_TILE_MID, s)
    tile = min(ATTN_TILE, s)
    row = lambda vec: vec.reshape(1, -1)
    bf = lambda w: w.astype(BF16)

    r, k, v, gate, lw, a = _rwkv_in(
        x, row(rw_norm[0]), rw_mu[0], bf(rw_w_in[0]), row(rw_w0[0]), bf(rw_w1[0]), bf(rw_w2[0]),
        row(rw_a0[0]), bf(rw_a1[0]), bf(rw_a2[0]), tm)
    y = _rwkv_scan(r, k, v, lw, a, row(rw_k_k[0]), row(rw_k_a[0]), row(rw_r_k[0]), row(rw_ln_w[0]),
                   row(rw_ln_b[0]))
    cos, sin = _rope_tables(s)
    h1, qa, qb, kq, vq, gate2 = _mid(
        x, y, gate, p, 0, bf(rw_w_out[0]), row(pe_norm[0]), bf(pe_w_gate[0]), bf(pe_w_proj[0]),
        row(da_norm[0]), bf(da_w_in[0]), cos, sin, tm_mid)
    lambda_init = 0.8 - 0.6 * math.exp(-0.3 * 1)
    o = _attn(qa, qb, kq, vq, row(da_lq1[0]), row(da_lk1[0]), row(da_lq2[0]), row(da_lk2[0]),
              row(da_subln[0]), lambda_init, tile)
    return _tail(h1, o, gate2, p, 1, bf(da_w_out[0]), row(pe_norm[1]), bf(pe_w_gate[1]),
                 bf(pe_w_proj[1]), row(final_norm), tm)
```

```python
import functools
import math

import jax
import jax.numpy as jnp
import numpy as np
from jax import lax
from jax.experimental import pallas as pl
from jax.experimental.pallas import tpu as pltpu

F32 = jnp.float32
BF16 = jnp.bfloat16

LANES = 128
MXU_WIDTH = 256
NORM_EPS = 1e-6
GN_EPS = 64e-5
DA_SUBLN_EPS = 1e-5
ROPE_THETA = 10000.0
RW_HEAD = 64
DA_QK_DIM = 64
DA_HEADS = 8
LORA = 64
ROW_TILE = 512
ROW_TILE_MID = 512
SCAN_CHUNK = 64
ATTN_TILE = 1024
ATTN_Q_BLOCK = 256
ATTN_K_BLOCK = 512
ATTN_AHEAD_SCORES = 3
ATTN_AHEAD_SOFTMAX = 1
ATTN_ONES_ROWS = 16
ATTN_HEADS_PER_STEP = 8
VMEM_LIMIT = 56 * 1024 * 1024
EXP_NEG_HALF = math.exp(-0.5)


def _cparams(sem):
    return pltpu.CompilerParams(dimension_semantics=sem, vmem_limit_bytes=VMEM_LIMIT)


def _resident(arr):
    return pl.BlockSpec(arr.shape, lambda *_: (0,) * arr.ndim, pipeline_mode=pl.Buffered(1))


def _rms(x, g, eps):
    return x * lax.rsqrt(jnp.mean(x * x, axis=-1, keepdims=True) + eps) * g


def _sigmoid(x):
    return 1.0 / (1.0 + jnp.exp(-x))


def _bdot(a, b):
    return jnp.dot(a.astype(BF16), b.astype(BF16), preferred_element_type=F32)


def _rwkv_in_kernel(x_ref, xp_ref, nrm_ref, mu_ref, win_ref, w0_ref, w1_ref, w2_ref,
                    a0_ref, a1_ref, a2_ref,
                    r_ref, k_ref, v_ref, g_ref, lw_ref, a_ref):
    i = pl.program_id(1)
    tm = x_ref.shape[1]
    c = r_ref.shape[2]
    nrm = nrm_ref[...]
    hn = _rms(x_ref[0], nrm, NORM_EPS)
    prev_row = _rms(xp_ref[0], nrm, NORM_EPS)[7:8, :]
    prev_row = jnp.where(i == 0, 0.0, prev_row)
    rows = lax.broadcasted_iota(jnp.int32, hn.shape, 0)
    shifted = jnp.where(rows == 0, prev_row, pltpu.roll(hn, 1, 0))
    dx = shifted - hn
    mu = mu_ref[...]

    def lerp(j):
        return (hn + dx * mu[j:j + 1, :]).astype(BF16)

    r_ref[0] = jnp.dot(lerp(0), win_ref[:, 0 * c:1 * c], preferred_element_type=F32)
    k_ref[0] = jnp.dot(lerp(1), win_ref[:, 1 * c:2 * c], preferred_element_type=F32)
    v_ref[0] = jnp.dot(lerp(2), win_ref[:, 2 * c:3 * c], preferred_element_type=F32)
    g_ref[0] = jnp.dot(lerp(3), win_ref[:, 3 * c:4 * c], preferred_element_type=F32)
    tw = jnp.tanh(jnp.dot(lerp(4), w1_ref[...], preferred_element_type=F32))
    zw = w0_ref[...] + _bdot(tw, w2_ref[...])
    lw_ref[0] = -EXP_NEG_HALF * _sigmoid(zw)
    ta = jnp.dot(lerp(5), a1_ref[...], preferred_element_type=F32)
    a_ref[0] = _sigmoid(a0_ref[...] + _bdot(ta, a2_ref[...]))


def _rwkv_in(x, nrm, mu, win, w0, w1, w2, a0, a1, a2, tm):
    b, s, d = x.shape
    c = win.shape[1] // 4
    full = _resident
    out_sds = jax.ShapeDtypeStruct((b, s, c), F32)
    out_spec = pl.BlockSpec((1, tm, c), lambda bi, i: (bi, i, 0))
    return pl.pallas_call(
        _rwkv_in_kernel,
        grid=(b, s // tm),
        in_specs=[
            pl.BlockSpec((1, tm, d), lambda bi, i: (bi, i, 0)),
            pl.BlockSpec((1, 8, d), lambda bi, i: (bi, jnp.maximum(i * (tm // 8) - 1, 0), 0)),
            full(nrm), full(mu), full(win), full(w0), full(w1), full(w2), full(a0), full(a1), full(a2),
        ],
        out_specs=[out_spec] * 6,
        out_shape=[out_sds] * 6,
        compiler_params=_cparams(("parallel", "parallel")),
        name="rwkv_in",
    )(x, x, nrm, mu, win, w0, w1, w2, a0, a1, a2)


def _hi_lo(x):
    hi = x.astype(BF16)
    lo = (x - hi.astype(F32)).astype(BF16)
    return hi, lo


def _cat2(xs, ys, axis):
    return (jnp.concatenate([xs[0], ys[0]], axis=axis), jnp.concatenate([xs[1], ys[1]], axis=axis))


def _dot3(a, b, dims):
    dn = (((dims[0],), (dims[1],)), ((), ()))
    free = 1 - dims[1]
    n = b[0].shape[free]
    if n <= MXU_WIDTH // 2:
        lhs = jnp.concatenate([a[0], a[1]], axis=dims[0])
        top = jnp.concatenate([b[0], b[1]], axis=free)
        bot = jnp.concatenate([b[0], jnp.zeros_like(b[0])], axis=free)
        out = lax.dot_general(lhs, jnp.concatenate([top, bot], axis=dims[1]), dn,
                              preferred_element_type=F32)
        return out[:, :n] + out[:, n:]
    lhs = jnp.concatenate([a[0], a[1], a[0]], axis=dims[0])
    rhs = jnp.concatenate([b[0], b[0], b[1]], axis=dims[1])
    return lax.dot_general(lhs, rhs, dn, preferred_element_type=F32)


def _mm(a, b):
    return _dot3(a, b, (1, 0))


def _mm_nt(a, b):
    return _dot3(a, b, (1, 1))


def _mm_tn(a, b):
    return _dot3(a, b, (0, 0))


def _scan_kernel(r_ref, k_ref, v_ref, lw_ref, a_ref, kk_ref, ka_ref, rk_ref, lnw_ref, lnb_ref,
                 y_ref, st_ref):
    nb, t, c = r_ref.shape
    ppb = c // LANES
    npairs = nb * ppb
    t2 = 2 * t

    @pl.when(pl.program_id(0) == 0)
    def _():
        st_ref[...] = jnp.zeros_like(st_ref)

    wide = lambda ref: jnp.concatenate([ref[b] for b in range(nb)], axis=1)
    tiled = lambda ref: jnp.concatenate([ref[...]] * nb, axis=1)
    r = wide(r_ref)
    kr = wide(k_ref)
    v = wide(v_ref)
    lw = wide(lw_ref)
    a = wide(a_ref)
    lnw = tiled(lnw_ref)
    lnb = tiled(lnb_ref)

    row_t = lax.broadcasted_iota(jnp.int32, (t, t), 0)
    col_t = lax.broadcasted_iota(jnp.int32, (t, t), 1)
    tri = (row_t >= col_t).astype(BF16)
    lw_h = lw.astype(BF16)
    lw_r = lw - lw_h.astype(F32)
    lw_m = lw_r.astype(BF16)
    lw_l = (lw_r - lw_m.astype(F32)).astype(BF16)
    cum = (jnp.dot(tri, lw_h, preferred_element_type=F32) + jnp.dot(tri, lw_m, preferred_element_type=F32)
           + jnp.dot(tri, lw_l, preferred_element_type=F32))
    cum_last = cum[t - 1:t, :]
    e_cum = jnp.exp(cum)
    e_prev = jnp.exp(cum - lw)
    e_inv = jnp.exp(-cum)
    e_tail = jnp.exp(cum_last - cum)
    e_last = jnp.exp(cum_last)
    kku = kr * tiled(kk_ref)
    km = kr * (1.0 + (a - 1.0) * tiled(ka_ref))
    rkr = r * km * tiled(rk_ref)
    r_dec = r * e_cum

    lane = lax.broadcasted_iota(jnp.int32, (1, LANES), 1)
    first = lane < RW_HEAD
    rl = lax.broadcasted_iota(jnp.int32, (LANES, LANES), 0)
    cl = lax.broadcasted_iota(jnp.int32, (LANES, LANES), 1)
    same_head = (rl >> 6) == (cl >> 6)
    head_ones = same_head.astype(BF16)
    head_ones2 = jnp.concatenate([head_ones, head_ones], axis=0)

    def head_sum(x):
        xh, xl = _hi_lo(x)
        return jnp.dot(jnp.concatenate([xh, xl], axis=1), head_ones2, preferred_element_type=F32)

    r2 = lax.broadcasted_iota(jnp.int32, (t2, t2), 0)
    c2 = lax.broadcasted_iota(jnp.int32, (t2, t2), 1)
    tb = t.bit_length() - 1
    strict_bd = ((r2 & (t - 1)) > (c2 & (t - 1))) & ((r2 >> tb) == (c2 >> tb))
    blk16 = (r2 >> 4) == (c2 >> 4)
    blk32 = (r2 >> 5) == (c2 >> 5)
    eye = (r2 == c2).astype(F32)
    r1 = lax.broadcasted_iota(jnp.int32, (t, t2), 0)
    c1 = lax.broadcasted_iota(jnp.int32, (t, t2), 1)
    incl = r1 >= (c1 & (t - 1))

    zero = jnp.zeros((), BF16)

    def stack(x):
        return tuple(jnp.concatenate([jnp.where(first, h, zero), jnp.where(first, zero, h)], axis=0)
                     for h in x)

    prs = range(npairs)
    sls = [slice(p * LANES, (p + 1) * LANES) for p in prs]
    each = lambda f, *xs: [f(*args) for args in zip(*xs)]

    sums = [head_sum(jnp.concatenate([kku[:, sl] * kku[:, sl], rkr[:, sl]], axis=0)) for sl in sls]
    kk = [kku[:, sl] / jnp.maximum(jnp.sqrt(sm[:t]), 1e-12) for sl, sm in zip(sls, sums)]
    bonus = [sm[t:] * v[:, sl] for sl, sm in zip(sls, sums)]
    bb = [kk_p * a[:, sl] for sl, kk_p in zip(sls, kk)]
    v_p = [_hi_lo(v[:, sl]) for sl in sls]
    r_p = [_hi_lo(r_dec[:, sl]) for sl in sls]
    kk_p = [_hi_lo(kk_x * e_prev[:, sl]) for sl, kk_x in zip(sls, kk)]
    kk_st = each(stack, kk_p)
    v_st = each(stack, v_p)
    kb_hat = [_cat2(stack(_hi_lo(km[:, sl] * e_inv[:, sl])), stack(_hi_lo(bb_p * e_inv[:, sl])), 0)
              for sl, bb_p in zip(sls, bb)]
    kb_bar = [_hi_lo(jnp.concatenate([km[:, sl] * e_tail[:, sl], -(bb_p * e_tail[:, sl])], axis=0))
              for sl, bb_p in zip(sls, bb)]

    g = [_mm_nt(_cat2(kk_p[p], r_p[p], 0), kb_hat[p]) for p in prs]
    twice = lambda x: jnp.concatenate([x, x], axis=0)
    a_k = [_hi_lo(jnp.concatenate([jnp.where(strict_bd, twice(g_p[:t, :t2]), 0.0),
                                   jnp.where(incl, g_p[t:, :t2], 0.0)], axis=0)) for g_p in g]
    a_kb = [jnp.where(strict_bd, twice(g_p[:t, t2:]), 0.0) for g_p in g]
    a_rb = [_hi_lo(jnp.where(incl, g_p[t:, t2:], 0.0)) for g_p in g]

    n1 = [jnp.where(blk16, x, 0.0) for x in a_kb]
    n1s = each(_hi_lo, n1)
    ns = [_hi_lo(_mm(x, x)) for x in n1s]
    m = [eye - x for x in n1]
    for _ in range(2):
        prod = [_mm(_cat2(_hi_lo(m_p), n_p, 0), n_p) for m_p, n_p in zip(m, ns)]
        m = [m_p + x[:t2] for m_p, x in zip(m, prod)]
        ns = [_hi_lo(x[t2:]) for x in prod]
    m = [m_p + _mm(_hi_lo(m_p), n_p) for m_p, n_p in zip(m, ns)]
    for hb, sel in ((16, blk32 & jnp.logical_not(blk16)), (32, jnp.logical_not(blk32))):
        lower = lambda x: jnp.concatenate([x[i:i + hb] for i in range(hb, t2, 2 * hb)], axis=0)
        ms = each(_hi_lo, m)
        es = [_hi_lo(jnp.where(sel, x, 0.0)) for x in a_kb]
        me = [_hi_lo(_mm(_hi_lo(lower(m_p)), e_p)) for m_p, e_p in zip(m, es)]
        mem = [_mm(me_p, ms_p) for me_p, ms_p in zip(me, ms)]
        pad = jnp.zeros((hb, t2), F32)
        spread = lambda x: jnp.concatenate(
            [blk for j in range(0, t, hb) for blk in (pad, x[j:j + hb])], axis=0)
        m = [m_p - spread(x) for m_p, x in zip(m, mem)]
    ms = each(_hi_lo, m)

    av = [_mm(a_k[p], v_st[p]) for p in prs]
    w = [_mm(ms[p], _cat2(_hi_lo(av[p][:t2]), kk_st[p], 1)) for p in prs]
    ht = [_hi_lo(st_ref[p]) for p in prs]
    sh = [_mm_nt(_cat2(_hi_lo(w[p][:, LANES:]), r_p[p], 0), ht[p]) for p in prs]
    u_st = [w[p][:, :LANES] + sh[p][:t2] for p in prs]
    u = [x[:t] + x[t:] for x in u_st]
    upd = [_mm_tn(_cat2(v_p[p], _hi_lo(u[p]), 0), kb_bar[p]) for p in prs]
    for p in prs:
        st_ref[p] = st_ref[p] * e_last[:, sls[p]] + jnp.where(same_head, upd[p], 0.0)
    y = [sh[p][t2:] + av[p][t2:] - _mm(a_rb[p], _hi_lo(u_st[p])) for p in prs]

    mean = [head_sum(x) * (1.0 / RW_HEAD) for x in y]
    d = [x - mu for x, mu in zip(y, mean)]
    var = [head_sum(x * x) * (1.0 / RW_HEAD) for x in d]
    for p in prs:
        sl = sls[p]
        yn = d[p] * lax.rsqrt(var[p] + GN_EPS) * lnw[:, sl] + lnb[:, sl]
        y_ref[p // ppb, :, sls[p % ppb]] = yn + bonus[p]


def _rwkv_scan(r, k, v, lw, a, k_k, k_a, r_k, ln_w, ln_b):
    b, s, c = r.shape
    t = SCAN_CHUNK
    seq = pl.BlockSpec((b, t, c), lambda i: (0, i, 0))
    par = pl.BlockSpec((1, c), lambda i: (0, 0))
    return pl.pallas_call(
        _scan_kernel,
        grid=(s // t,),
        in_specs=[seq] * 5 + [par] * 5,
        out_specs=seq,
        out_shape=jax.ShapeDtypeStruct((b, s, c), F32),
        scratch_shapes=[pltpu.VMEM((b * c // LANES, LANES, LANES), F32)],
        compiler_params=_cparams(("arbitrary",)),
        name="rwkv_scan",
    )(r, k, v, lw, a, k_k, k_a, r_k, ln_w, ln_b)


def _ple(h, p, nrm, wg, wp):
    g = _sigmoid(_bdot(_rms(h, nrm, NORM_EPS), wg))
    return h + g * _bdot(p, wp)


def _mid_kernel(x_ref, y_ref, g_ref, p_ref, wo_ref, pn_ref, pg_ref, pp_ref, dn_ref, din_ref,
                cos_ref, sin_ref,
                h_ref, qa_ref, qb_ref, k_ref, v_ref, go_ref):
    nh = qa_ref.shape[1]
    gate = g_ref[0]
    z = y_ref[0] * (gate * _sigmoid(gate))
    h = x_ref[0] + _bdot(z, wo_ref[...])
    h = _ple(h, p_ref[0, 0], pn_ref[...], pg_ref[...], pp_ref[...])
    h_ref[0] = h
    proj = _bdot(_rms(h, dn_ref[...], NORM_EPS), din_ref[...])
    cw = nh * LANES
    cos = cos_ref[...]
    sin = sin_ref[...]
    lane = lax.broadcasted_iota(jnp.int32, (1, LANES), 1)
    low = (lane & (DA_QK_DIM - 1)) < (DA_QK_DIM // 2)
    first = lane < DA_QK_DIM
    scale = DA_QK_DIM ** -0.5 * math.log2(math.e)
    ones = jnp.ones((ATTN_ONES_ROWS, proj.shape[0]), BF16)

    def rope(xh):
        rot = jnp.where(low, pltpu.roll(xh, LANES - DA_QK_DIM // 2, 1), pltpu.roll(xh, DA_QK_DIM // 2, 1))
        return xh * cos + rot * sin

    for hd in range(nh):
        sl = slice(hd * LANES, (hd + 1) * LANES)
        q = rope(proj[:, sl]) * scale
        qa_ref[0, hd] = jnp.where(first, q, 0.0).astype(BF16)
        qb_ref[0, hd] = jnp.where(first, 0.0, q).astype(BF16)
        k_ref[0, hd] = rope(proj[:, cw + hd * LANES:cw + (hd + 1) * LANES]).astype(BF16)
        v_ref[0, hd, :LANES, :] = proj[:, 2 * cw + hd * LANES:2 * cw + (hd + 1) * LANES].T.astype(BF16)
        v_ref[0, hd, LANES:, :] = ones
    go_ref[0] = proj[:, 3 * cw:]


def _mid(x, y, g, p, layer, wo, pn, pg, pp, dn, din, cos, sin, tm):
    b, s, d = x.shape
    nh = DA_HEADS
    p_spec = pl.BlockSpec((1, 1, tm, p.shape[3]), lambda bi, i: (layer, bi, i, 0))
    full = _resident
    row = lambda w: pl.BlockSpec((1, tm, w), lambda bi, i: (bi, i, 0))
    hd_spec = pl.BlockSpec((1, nh, tm, LANES), lambda bi, i: (bi, 0, i, 0))
    hd_sds = jax.ShapeDtypeStruct((b, nh, s, LANES), BF16)
    va_spec = pl.BlockSpec((1, nh, LANES + ATTN_ONES_ROWS, tm), lambda bi, i: (bi, 0, 0, i))
    va_sds = jax.ShapeDtypeStruct((b, nh, LANES + ATTN_ONES_ROWS, s), BF16)
    tab = pl.BlockSpec((tm, LANES), lambda bi, i: (i, 0))
    return pl.pallas_call(
        _mid_kernel,
        grid=(b, s // tm),
        in_specs=[row(d), row(d), row(d), p_spec, full(wo), full(pn), full(pg), full(pp), full(dn),
                  full(din), tab, tab],
        out_specs=[row(d), hd_spec, hd_spec, hd_spec, va_spec, row(d)],
        out_shape=[jax.ShapeDtypeStruct((b, s, d), F32), hd_sds, hd_sds, hd_sds, va_sds,
                   jax.ShapeDtypeStruct((b, s, d), F32)],
        compiler_params=_cparams(("parallel", "parallel")),
        name="mid",
    )(x, y, g, p, wo, pn, pg, pp, dn, din, cos, sin)


def _attn_kernel(qi_ref, ki_ref, qa_ref, qb_ref, k_ref, v_ref, lq1_ref, lk1_ref, lq2_ref, lk2_ref,
                 sub_ref, o_ref, m1_ref, acc1_ref, m2_ref, acc2_ref, s_ref, *, lambda_init, q_block, k_block):
    step = pl.program_id(2)
    qi = qi_ref[step]
    ki = ki_ref[step]
    nhs = qa_ref.shape[1]
    tq = qa_ref.shape[2]
    tk = k_ref.shape[2]

    @pl.when(ki == 0)
    def _():
        m1_ref[...] = jnp.full_like(m1_ref, -jnp.inf)
        m2_ref[...] = jnp.full_like(m2_ref, -jnp.inf)
        acc1_ref[...] = jnp.zeros_like(acc1_ref)
        acc2_ref[...] = jnp.zeros_like(acc2_ref)

    comps = ((qa_ref, m1_ref, acc1_ref), (qb_ref, m2_ref, acc2_ref))

    def scores(job):
        comp, hh, c0, k0 = job
        return lax.dot_general(k_ref[0, hh, pl.ds(k0, k_block), :], comps[comp][0][0, hh, pl.ds(c0, q_block), :],
                               (((1,), (1,)), ((), ())), preferred_element_type=F32)

    def softmax(job, s, masked):
        comp, hh, c0, k0 = job
        m_ref = comps[comp][1]
        cols = pl.ds(c0, q_block)
        if masked and k0 + k_block > c0 + 1:
            rk = lax.broadcasted_iota(jnp.int32, (k_block, q_block), 0) + k0
            cq = lax.broadcasted_iota(jnp.int32, (k_block, q_block), 1) + c0
            s = jnp.where(rk <= cq, s, -jnp.inf)
        m_old = m_ref[hh, :, cols]
        m_new = jnp.maximum(m_old, jnp.max(s, axis=0, keepdims=True))
        m_ref[hh, :, cols] = m_new
        alpha = jnp.exp2(m_old[:1] - m_new[:1])
        return jnp.exp2(s - m_new[:1]).astype(BF16), alpha

    def accumulate(job, pr, alpha):
        comp, hh, c0, k0 = job
        acc_ref = comps[comp][2]
        cols = pl.ds(c0, q_block)
        pv = jnp.dot(v_ref[0, hh, :, pl.ds(k0, k_block)], pr, preferred_element_type=F32)
        acc_ref[hh, :, cols] = alpha * acc_ref[hh, :, cols] + pv

    def tile_update(masked):
        jobs = [(comp, hh, c0, k0)
                for k0 in range(0, tk, k_block) for c0 in range(0, tq, q_block)
                for hh in range(nhs) for comp in range(2)
                if not (masked and k0 >= c0 + q_block)]
        n = len(jobs)
        p = {}
        nslots = s_ref.shape[0]
        for i in range(-ATTN_AHEAD_SCORES, n):
            js, jp = i + ATTN_AHEAD_SCORES, i + ATTN_AHEAD_SOFTMAX
            if js < n:
                s_ref[js % nslots] = scores(jobs[js])
            if 0 <= jp < n:
                p[jp] = softmax(jobs[jp], s_ref[jp % nslots], masked)
            if i >= 0:
                accumulate(jobs[i], *p.pop(i))

    @pl.when(ki < qi)
    def _():
        tile_update(False)

    @pl.when(ki == qi)
    def _():
        tile_update(True)
        lam = (jnp.exp(jnp.sum(lq1_ref[...] * lk1_ref[...], axis=-1, keepdims=True))
               - jnp.exp(jnp.sum(lq2_ref[...] * lk2_ref[...], axis=-1, keepdims=True)) + lambda_init)
        for hh in range(nhs):
            ot = (acc1_ref[hh, :LANES, :] / acc1_ref[hh, LANES:LANES + 1, :]
                  - lam * (acc2_ref[hh, :LANES, :] / acc2_ref[hh, LANES:LANES + 1, :]))
            o_ref[0, :, hh * LANES:(hh + 1) * LANES] = (_rms(ot.T, sub_ref[...], DA_SUBLN_EPS)
                                                        * (1.0 - lambda_init))


def _attn(qa, qb, k, vt, lq1, lk1, lq2, lk2, sub, lambda_init, tile):
    b, nh, dvr, s = vt.shape
    nhs = ATTN_HEADS_PER_STEP
    nq = s // tile
    pairs = [(q, kk) for q in range(nq) for kk in range(q + 1)]
    qi = jnp.asarray(np.array([pq for pq, _ in pairs], np.int32))
    ki = jnp.asarray(np.array([pk for _, pk in pairs], np.int32))
    qspec = pl.BlockSpec((1, nhs, tile, LANES), lambda bi, h, t, qi, ki: (bi, h, qi[t], 0))
    kspec = pl.BlockSpec((1, nhs, tile, LANES), lambda bi, h, t, qi, ki: (bi, h, ki[t], 0))
    vspec = pl.BlockSpec((1, nhs, dvr, tile), lambda bi, h, t, qi, ki: (bi, h, 0, ki[t]))
    small = lambda arr: pl.BlockSpec(arr.shape, lambda bi, h, t, qi, ki: (0,) * arr.ndim)
    grid_spec = pltpu.PrefetchScalarGridSpec(
        num_scalar_prefetch=2,
        grid=(b, nh // nhs, len(pairs)),
        in_specs=[qspec, qspec, kspec, vspec, small(lq1), small(lk1), small(lq2), small(lk2), small(sub)],
        out_specs=pl.BlockSpec((1, tile, nhs * LANES), lambda bi, h, t, qi, ki: (bi, qi[t], h)),
        scratch_shapes=[pltpu.VMEM((nhs, 8, tile), F32), pltpu.VMEM((nhs, dvr, tile), F32)] * 2
        + [pltpu.VMEM((ATTN_AHEAD_SCORES - ATTN_AHEAD_SOFTMAX + 1, min(ATTN_K_BLOCK, tile),
                       min(ATTN_Q_BLOCK, tile)), F32)],
    )
    return pl.pallas_call(
        functools.partial(_attn_kernel, lambda_init=lambda_init, q_block=min(ATTN_Q_BLOCK, tile),
                          k_block=min(ATTN_K_BLOCK, tile)),
        grid_spec=grid_spec,
        out_shape=jax.ShapeDtypeStruct((b, s, nh * LANES), F32),
        compiler_params=_cparams(("parallel", "parallel", "arbitrary")),
        name="diff_attn",
    )(qi, ki, qa, qb, k, vt, lq1, lk1, lq2, lk2, sub)


def _tail_kernel(h_ref, o_ref, g_ref, p_ref, wo_ref, pn_ref, pg_ref, pp_ref, fn_ref, out_ref):
    gate = g_ref[0]
    z = o_ref[0] * (gate * _sigmoid(gate))
    h = h_ref[0] + _bdot(z, wo_ref[...])
    h = _ple(h, p_ref[0, 0], pn_ref[...], pg_ref[...], pp_ref[...])
    out_ref[0] = _rms(h, fn_ref[...], NORM_EPS)


def _tail(h, o, g, p, layer, wo, pn, pg, pp, fn, tm):
    b, s, d = h.shape
    p_spec = pl.BlockSpec((1, 1, tm, p.shape[3]), lambda bi, i: (layer, bi, i, 0))
    full = _resident
    row = lambda w: pl.BlockSpec((1, tm, w), lambda bi, i: (bi, i, 0))
    return pl.pallas_call(
        _tail_kernel,
        grid=(b, s // tm),
        in_specs=[row(d), row(d), row(d), p_spec, full(wo), full(pn), full(pg), full(pp), full(fn)],
        out_specs=row(d),
        out_shape=jax.ShapeDtypeStruct((b, s, d), F32),
        compiler_params=_cparams(("parallel", "parallel")),
        name="tail",
    )(h, o, g, p, wo, pn, pg, pp, fn)


def _rope_tables(s):
    dk = DA_QK_DIM
    pos = jnp.arange(s, dtype=F32)
    inv_freq = 1.0 / (ROPE_THETA ** (jnp.arange(0, dk, 2, dtype=F32) / dk))
    ang = pos[:, None] * inv_freq[None, :]
    cos, sin = lax.optimization_barrier((jnp.cos(ang), jnp.sin(ang)))
    reps = LANES // dk
    return (jnp.concatenate([cos, cos] * reps, axis=-1),
            jnp.concatenate([-sin, sin] * reps, axis=-1))


def kernel(x, p, rw_norm, rw_mu, rw_w_in, rw_w0, rw_w1, rw_w2, rw_a0, rw_a1, rw_a2, rw_k_k, rw_k_a, rw_r_k, rw_ln_w, rw_ln_b, rw_w_out, da_norm, da_w_in, da_lq1, da_lk1, da_lq2, da_lk2, da_subln, da_w_out, pe_norm, pe_w_gate, pe_w_proj, final_norm):
    b, s, d = x.shape
    assert p.shape[0] == 2 and rw_norm.shape[0] == 1 and da_norm.shape[0] == 1
    tm = min(ROW_TILE, s)
    tm_mid = min(ROW_TILE_MID, s)
    tile = min(ATTN_TILE, s)
    row = lambda vec: vec.reshape(1, -1)
    bf = lambda w: w.astype(BF16)

    r, k, v, gate, lw, a = _rwkv_in(
        x, row(rw_norm[0]), rw_mu[0], bf(rw_w_in[0]), row(rw_w0[0]), bf(rw_w1[0]), bf(rw_w2[0]),
        row(rw_a0[0]), bf(rw_a1[0]), bf(rw_a2[0]), tm)
    y = _rwkv_scan(r, k, v, lw, a, row(rw_k_k[0]), row(rw_k_a[0]), row(rw_r_k[0]), row(rw_ln_w[0]),
                   row(rw_ln_b[0]))
    cos, sin = _rope_tables(s)
    h1, qa, qb, kq, vq, gate2 = _mid(
        x, y, gate, p, 0, bf(rw_w_out[0]), row(pe_norm[0]), bf(pe_w_gate[0]), bf(pe_w_proj[0]),
        row(da_norm[0]), bf(da_w_in[0]), cos, sin, tm_mid)
    lambda_init = 0.8 - 0.6 * math.exp(-0.3 * 1)
    o = _attn(qa, qb, kq, vq, row(da_lq1[0]), row(da_lk1[0]), row(da_lq2[0]), row(da_lk2[0]),
              row(da_subln[0]), lambda_init, tile)
    return _tail(h1, o, gate2, p, 1, bf(da_w_out[0]), row(pe_norm[1]), bf(pe_w_gate[1]),
                 bf(pe_w_proj[1]), row(final_norm), tm)
```
